```python
import jax, jax.numpy as jnp
from jax import lax
import numpy as np

D_MODEL = 2048
BATCH = 2
SEQ = 8192
DEPTH = 4

CHUNK = 64
EPS = 1e-6
MIN_FORGET = 1e-30
A_VAL = 128
A_KEY = 128
A_VW = D_MODEL // 2
A_HEADS = A_VW // A_VAL
A_KW = A_HEADS * A_KEY
B_WIDTH = D_MODEL // 2
B_WINDOWS = (2, 4, 8, 16)
B_GROUPS = len(B_WINDOWS)
B_GW = B_WIDTH // B_GROUPS
IN_SIZES = (A_KW, A_KW, A_VW, A_VW, B_WIDTH, B_WIDTH, D_MODEL, D_MODEL)
IN_COLS = sum(IN_SIZES)
IN_SPLITS = [int(v) for v in np.cumsum(IN_SIZES)[:-1]]

kernel_name = "hybrid_hgrn2_pool_gated_trunk"


def rmsnorm(x, gain):
    xf = x.astype(jnp.float32)
    y = xf * lax.rsqrt(jnp.mean(xf * xf, axis=-1, keepdims=True) + EPS)
    return y.astype(x.dtype) * gain


def hgrn2_mixer(q_raw, f_raw, v, lb):
    f32 = jnp.float32
    bsz, seq, _ = q_raw.shape
    nc = seq // CHUNK
    q = jax.nn.silu(q_raw.astype(f32))
    a = f_raw.astype(f32)
    lb = lb.astype(f32)
    f = lb + (1.0 - lb) * jax.nn.sigmoid(a)
    log_f = jnp.log(jnp.maximum(f, MIN_FORGET))
    k = (1.0 - lb) * jax.nn.sigmoid(-a)
    v = v.astype(f32)

    def to_chunks(t, d):
        return t.reshape(bsz, nc, CHUNK, A_HEADS, d).transpose(1, 0, 3, 2, 4)

    qc, kc, lfc = to_chunks(q, A_KEY), to_chunks(k, A_KEY), to_chunks(log_f, A_KEY)
    vc = to_chunks(v, A_VAL)
    causal = jnp.tril(jnp.ones((CHUNK, CHUNK), dtype=bool))[:, :, None]
    causal_f = causal.astype(f32)

    def step(state, inp):
        qi, ki, vi, lfi = inp
        b = jnp.cumsum(lfi, axis=2)
        o_inter = jnp.einsum('bhtk,bhkv->bhtv', qi * jnp.exp(b), state)
        diff = b[:, :, :, None, :] - b[:, :, None, :, :]
        decay = jnp.exp(jnp.where(causal, diff, 0.0)) * causal_f
        scores = jnp.einsum('bhtk,bhtsk,bhsk->bhts', qi, decay, ki)
        o = o_inter + jnp.einsum('bhts,bhsv->bhtv', scores, vi)
        b_last = b[:, :, -1:, :]
        state = (jnp.exp(b_last[:, :, 0, :])[..., None] * state
                 + jnp.einsum('bhsk,bhsv->bhkv', ki * jnp.exp(b_last - b), vi))
        return state, o

    state0 = jnp.zeros((bsz, A_HEADS, A_KEY, A_VAL), f32)
    _, o = lax.scan(step, state0, (qc, kc, vc, lfc))
    return o.transpose(1, 0, 3, 2, 4).reshape(bsz, seq, A_HEADS, A_VAL)


def pool_mixer(v, pool_w, pool_scale):
    f32 = jnp.float32
    bsz, seq, _ = v.shape
    vf = v.astype(f32)
    cs = jnp.concatenate([jnp.zeros((bsz, 1, B_WIDTH), f32), lax.cumsum(vf, axis=1)], axis=1)
    pos = jnp.arange(1, seq + 1, dtype=f32)[None, :, None]
    outs = []
    for g, w in enumerate(B_WINDOWS):
        sl = slice(g * B_GW, (g + 1) * B_GW)
        cs_g = cs[:, :, sl]
        shifted = jnp.pad(cs_g, ((0, 0), (w, 0), (0, 0)))[:, :seq + 1]
        mean = (cs_g - shifted)[:, 1:] / jnp.minimum(pos, float(w))
        outs.append(mean - vf[:, :, sl])
    pooled = jnp.stack(outs, axis=2)
    mixed = jnp.einsum('bsgc,gcd->bsgd', pooled, pool_w.astype(f32)).reshape(bsz, seq, B_WIDTH)
    return (mixed * pool_scale.astype(f32)).astype(v.dtype)


def setup_inputs(seed: int = 0) -> dict:
    key = jax.random.key(seed)
    ks = jax.random.split(key, 16)
    nrm = jax.random.normal
    f32 = jnp.float32
    return {
        "x": nrm(ks[0], (BATCH, SEQ, D_MODEL), f32),
        "c": nrm(ks[1], (BATCH, D_MODEL), f32),
        "w_ada": nrm(ks[2], (DEPTH, D_MODEL, 3 * D_MODEL), f32) * (0.5 * D_MODEL ** -0.5),
        "b_ada": nrm(ks[3], (DEPTH, 3 * D_MODEL), f32) * 0.02,
        "norm_pre": 1.0 + 0.1 * nrm(ks[4], (DEPTH, D_MODEL), f32),
        "norm_post": 1.0 + 0.1 * nrm(ks[5], (DEPTH, D_MODEL), f32),
        "w_in": nrm(ks[6], (DEPTH, D_MODEL, IN_COLS), f32) * D_MODEL ** -0.5,
        "lower_bounds": nrm(ks[7], (DEPTH, A_KW), f32),
        "hgrn_norm": 1.0 + 0.1 * nrm(ks[8], (DEPTH, A_VW), f32),
        "pool_w": nrm(ks[9], (DEPTH, B_GROUPS, B_GW, B_GW), f32) * B_GW ** -0.5,
        "pool_scale": 1.0 + 0.1 * nrm(ks[10], (DEPTH, B_WIDTH), f32),
        "w_proj_a": nrm(ks[11], (DEPTH, A_VW, D_MODEL), f32) * A_VW ** -0.5,
        "w_proj_b": nrm(ks[12], (DEPTH, B_WIDTH, D_MODEL), f32) * B_WIDTH ** -0.5,
        "w_out": nrm(ks[13], (DEPTH, D_MODEL, D_MODEL), f32) * D_MODEL ** -0.5,
    }


def reference(x, c, w_ada, b_ada, norm_pre, norm_post, w_in, lower_bounds, hgrn_norm,
              pool_w, pool_scale, w_proj_a, w_proj_b, w_out):
    bsz, seq, _ = x.shape
    sm = jax.nn.softmax(lower_bounds.astype(jnp.float32), axis=0)
    lb_all = jnp.cumsum(sm, axis=0) - sm[0:1]
    c_act = jax.nn.silu(c)
    for l in range(DEPTH):
        mod = c_act @ w_ada[l] + b_ada[l]
        shift, scale, gate = jnp.split(mod, 3, axis=-1)
        h = rmsnorm(x, norm_pre[l]) * (1.0 + scale[:, None, :]) + shift[:, None, :]
        proj = h @ w_in[l]
        q_a, f_a, v_a, z_a, v_b, z_b, g_a, g_b = jnp.split(proj, IN_SPLITS, axis=-1)
        o_a = hgrn2_mixer(q_a, f_a, v_a, lb_all[l])
        o_a = o_a * lax.rsqrt(jnp.mean(o_a * o_a, axis=-1, keepdims=True) + EPS)
        y_a = o_a.reshape(bsz, seq, A_VW).astype(x.dtype) * hgrn_norm[l] * jax.nn.silu(z_a)
        y_b = pool_mixer(v_b, pool_w[l], pool_scale[l]) * jax.nn.silu(z_b)
        merged = jax.nn.sigmoid(g_a) * (y_a @ w_proj_a[l]) + jax.nn.sigmoid(g_b) * (y_b @ w_proj_b[l])
        out = merged @ w_out[l]
        x = x + gate[:, None, :] * rmsnorm(out, norm_post[l])
    return x
```

```python
from functools import partial

import jax
import jax.numpy as jnp
from jax import lax
from jax.experimental import pallas as pl
from jax.experimental.pallas import tpu as pltpu

D_MODEL = 2048
DEPTH = 4
CHUNK = 64
EPS = 1e-6
MIN_FORGET = 1e-30
HEAD_DIM = 128
A_WIDTH = D_MODEL // 2
N_HEADS = A_WIDTH // HEAD_DIM
B_WIDTH = D_MODEL // 2
POOL_WINDOWS = (2, 4, 8, 16)
POOL_GW = B_WIDTH // len(POOL_WINDOWS)
POOL_HALO = 16
IN_COLS = 6 * A_WIDTH + 2 * D_MODEL
COL_BLK = 1024
CB_Q, CB_F, CB_VA, CB_ZA, CB_VB, CB_ZB = 0, 1, 2, 3, 4, 5
GB_A, GB_B = 3, 4

SAFE_CHUNK_LOG_DECAY = 120.0

IN_TM = 1024
MIX_TC = 256
OUT_TM = 512
VMEM_LIMIT = 56 * 1024 * 1024

F32 = jnp.float32
BF16 = jnp.bfloat16


def _sigmoid(v):
    return 1.0 / (1.0 + jnp.exp(-v))


def _dot(a, b):
    return jnp.dot(a, b, preferred_element_type=F32)


def _dot_nt(a, b):
    return lax.dot_general(a, b, (((1,), (1,)), ((), ())), preferred_element_type=F32)


def _dot_tn(a, b):
    return lax.dot_general(a, b, (((0,), (0,)), ((), ())), preferred_element_type=F32)


def _ada_kernel(ct_ref, w_ref, b_ref, o_ref):
    ct = ct_ref[...]
    ca = ct * _sigmoid(ct)
    w = w_ref[...]
    rows = [jnp.sum(w * ca[:, b:b + 1], axis=0, keepdims=True)
            for b in range(ct.shape[1])]
    o_ref[...] = jnp.concatenate(rows, axis=0) + b_ref[...]


def _ada_call(c, w_ada, b_ada):
    bsz = c.shape[0]
    tn = 1024
    n3 = 3 * D_MODEL
    return pl.pallas_call(
        _ada_kernel,
        out_shape=jax.ShapeDtypeStruct((DEPTH, bsz, n3), F32),
        grid=(DEPTH, n3 // tn),
        in_specs=[
            pl.BlockSpec((D_MODEL, bsz), lambda l, j: (0, 0)),
            pl.BlockSpec((None, D_MODEL, tn), lambda l, j: (l, 0, j)),
            pl.BlockSpec((None, 1, tn), lambda l, j: (l, 0, j)),
        ],
        out_specs=pl.BlockSpec((None, bsz, tn), lambda l, j: (l, 0, j)),
        compiler_params=pltpu.CompilerParams(
            dimension_semantics=("arbitrary", "arbitrary"),
            vmem_limit_bytes=VMEM_LIMIT),
        name="ada_mod",
    )(c.T, w_ada, b_ada.reshape(DEPTH, 1, n3))


def _inproj_kernel(x_ref, gain_ref, shift_ref, scale_ref, w_ref, o_ref, f_ref, h_ref):
    j = pl.program_id(1)

    @pl.when(j == 0)
    def _():
        x = x_ref[...]
        inv = lax.rsqrt(jnp.mean(x * x, axis=-1, keepdims=True) + EPS)
        h = (x * inv) * gain_ref[...]
        h = h * (1.0 + scale_ref[...]) + shift_ref[...]
        h_ref[...] = h.astype(BF16)

    acc = _dot(h_ref[...], w_ref[...])
    o_ref[...] = acc.astype(BF16)

    @pl.when(j == CB_F)
    def _():
        f_ref[...] = acc


def _inproj_call(layer, x2, mod5, norm_pre, w_in_b, tiles_per_batch):
    t = x2.shape[0]
    nj = IN_COLS // COL_BLK

    def mod_spec(which):
        return pl.BlockSpec((None, None, None, 1, D_MODEL),
                            lambda i, j: (layer, i // tiles_per_batch, which, 0, 0))

    return pl.pallas_call(
        _inproj_kernel,
        out_shape=(jax.ShapeDtypeStruct((t, IN_COLS), BF16),
                   jax.ShapeDtypeStruct((t, A_WIDTH), F32)),
        grid=(t // IN_TM, nj),
        in_specs=[
            pl.BlockSpec((IN_TM, D_MODEL), lambda i, j: (i, 0)),
            pl.BlockSpec((None, 1, D_MODEL), lambda i, j: (layer, 0, 0)),
            mod_spec(0), mod_spec(1),
            pl.BlockSpec((None, D_MODEL, COL_BLK), lambda i, j: (layer, 0, j)),
        ],
        out_specs=(pl.BlockSpec((IN_TM, COL_BLK), lambda i, j: (i, j)),
                   pl.BlockSpec((IN_TM, A_WIDTH), lambda i, j: (i, 0))),
        scratch_shapes=[pltpu.VMEM((IN_TM, D_MODEL), BF16)],
        compiler_params=pltpu.CompilerParams(
            dimension_semantics=("arbitrary", "arbitrary"),
            vmem_limit_bytes=VMEM_LIMIT),
        name="in_proj",
    )(x2, norm_pre.reshape(DEPTH, 1, D_MODEL), mod5, mod5, w_in_b)


def _split3(v):
    hi = v.astype(BF16)
    r1 = v - hi.astype(F32)
    mid = r1.astype(BF16)
    lo = (r1 - mid.astype(F32)).astype(BF16)
    return hi, mid, lo


def _mixer_kernel(coef_ref, lbraw_ref, hn_ref, pscale_ref, q_ref, f_ref, va_ref, za_ref,
                  vb_ref, zb_ref, pw_ref, ya_ref, yb_ref,
                  st_ref, ext_ref, a_ref, bs_ref, qs_ref, ks_ref):
    i = pl.program_id(1)
    tc = q_ref.shape[0]

    @pl.when(i == 0)
    def _():
        st_ref[...] = jnp.zeros_like(st_ref)
        ext_ref[0:POOL_HALO, :] = jnp.zeros((POOL_HALO, B_WIDTH), F32)

    lbr = lbraw_ref[...]
    e = jnp.exp(lbr - jnp.max(lbr, axis=0, keepdims=True))
    sm = e / jnp.sum(e, axis=0, keepdims=True)
    lb = jnp.sum(coef_ref[...] * sm, axis=0, keepdims=True)
    one_m_lb = 1.0 - lb

    row = lax.broadcasted_iota(jnp.int32, (CHUNK, CHUNK), 0)
    col = lax.broadcasted_iota(jnp.int32, (CHUNK, CHUNK), 1)
    causal = row >= col
    tri = causal.astype(BF16)
    hn = hn_ref[...]

    def chunk_body(ci, carry):
        rows = pl.ds(pl.multiple_of(ci * CHUNK, CHUNK), CHUNK)
        a = f_ref[rows, :]
        sig = _sigmoid(a)
        f = lb + one_m_lb * sig
        lf = jnp.log(jnp.maximum(f, MIN_FORGET))
        kk = one_m_lb * (1.0 - sig)
        qr = q_ref[rows, :].astype(F32)
        q = qr * _sigmoid(qr)

        hi, mid, lo = _split3(lf)
        b = _dot(tri, hi) + _dot(tri, mid) + _dot(tri, lo)
        b_last = b[CHUNK - 1:CHUNK, :]
        half = 0.5 * b_last
        q_in = (q * jnp.exp(b - half)).astype(BF16)
        k_in = (kk * jnp.exp(half - b)).astype(BF16)
        q_st = (q * jnp.exp(b)).astype(BF16)
        k_st = (kk * jnp.exp(b_last - b)).astype(BF16)
        st_decay = jnp.exp(b_last)

        for h in range(N_HEADS):
            cs = slice(h * HEAD_DIM, (h + 1) * HEAD_DIM)
            sc = _dot_nt(q_in[:, cs], k_in[:, cs])
            a_ref[h] = jnp.where(causal, sc, 0.0)

        unsafe = jnp.min(b_last) < -SAFE_CHUNK_LOG_DECAY

        @pl.when(unsafe)
        def _():
            bs_ref[...] = b
            qs_ref[...] = q
            ks_ref[...] = kk

            def src_body(s, c2):
                brow = bs_ref[pl.ds(s, 1), :]
                krow = ks_ref[pl.ds(s, 1), :]
                p = jnp.exp(jnp.minimum(bs_ref[...] - brow, 0.0)) * qs_ref[...] * krow
                for h in range(N_HEADS):
                    cs = slice(h * HEAD_DIM, (h + 1) * HEAD_DIM)
                    colv = jnp.sum(p[:, cs], axis=1, keepdims=True)
                    a_ref[h] = jnp.where((col == s) & causal, colv, a_ref[h])
                return c2

            lax.fori_loop(0, CHUNK, src_body, 0)

        for h in range(N_HEADS):
            cs = slice(h * HEAD_DIM, (h + 1) * HEAD_DIM)
            v = va_ref[rows, cs]
            st = st_ref[h]
            o = _dot_nt(q_st[:, cs], st.astype(BF16)) + _dot(a_ref[h].astype(BF16), v)
            st_ref[h] = st * st_decay[:, cs] + _dot_tn(v, k_st[:, cs])
            o = o * lax.rsqrt(jnp.mean(o * o, axis=-1, keepdims=True) + EPS)
            z = za_ref[rows, cs].astype(F32)
            ya_ref[rows, cs] = (o * hn[:, cs] * (z * _sigmoid(z))).astype(BF16)
        return carry

    lax.fori_loop(0, tc // CHUNK, chunk_body, 0)

    ext_ref[POOL_HALO:POOL_HALO + tc, :] = vb_ref[...].astype(F32)
    pos = (i * tc + lax.broadcasted_iota(jnp.int32, (tc, POOL_GW), 0) + 1).astype(F32)
    for g, w in enumerate(POOL_WINDOWS):
        cs = slice(g * POOL_GW, (g + 1) * POOL_GW)
        cur = ext_ref[POOL_HALO:POOL_HALO + tc, cs]
        tot = cur
        for d in range(1, w):
            tot = tot + ext_ref[POOL_HALO - d:POOL_HALO - d + tc, cs]
        pooled = tot / jnp.minimum(pos, float(w)) - cur
        mixed = _dot(pooled.astype(BF16), pw_ref[g]) * pscale_ref[:, cs]
        z = zb_ref[:, cs].astype(F32)
        yb_ref[:, cs] = (mixed * (z * _sigmoid(z))).astype(BF16)
    ext_ref[0:POOL_HALO, :] = ext_ref[tc:tc + POOL_HALO, :]


def _mixer_call(layer, proj, fraw, coef, lower_bounds, hgrn_norm, pool_scale, pool_w_b,
                bsz, seq):
    t = proj.shape[0]
    nt = seq // MIX_TC

    def col_spec(cb):
        return pl.BlockSpec((MIX_TC, COL_BLK), lambda b, i: (b * nt + i, cb))

    small = lambda shape, imap: pl.BlockSpec(shape, imap)
    return pl.pallas_call(
        _mixer_kernel,
        out_shape=(jax.ShapeDtypeStruct((t, A_WIDTH), BF16),
                   jax.ShapeDtypeStruct((t, B_WIDTH), BF16)),
        grid=(bsz, nt),
        in_specs=[
            small((None, DEPTH, A_WIDTH), lambda b, i: (layer, 0, 0)),
            small((DEPTH, A_WIDTH), lambda b, i: (0, 0)),
            small((None, 1, A_WIDTH), lambda b, i: (layer, 0, 0)),
            small((None, 1, B_WIDTH), lambda b, i: (layer, 0, 0)),
            col_spec(CB_Q),
            pl.BlockSpec((MIX_TC, A_WIDTH), lambda b, i: (b * nt + i, 0)),
            col_spec(CB_VA), col_spec(CB_ZA), col_spec(CB_VB), col_spec(CB_ZB),
            small((None, len(POOL_WINDOWS), POOL_GW, POOL_GW), lambda b, i: (layer, 0, 0, 0)),
        ],
        out_specs=(pl.BlockSpec((MIX_TC, A_WIDTH), lambda b, i: (b * nt + i, 0)),
                   pl.BlockSpec((MIX_TC, B_WIDTH), lambda b, i: (b * nt + i, 0))),
        scratch_shapes=[
            pltpu.VMEM((N_HEADS, HEAD_DIM, HEAD_DIM), F32),
            pltpu.VMEM((MIX_TC + POOL_HALO, B_WIDTH), F32),
            pltpu.VMEM((N_HEADS, CHUNK, CHUNK), F32),
            pltpu.VMEM((CHUNK, A_WIDTH), F32),
            pltpu.VMEM((CHUNK, A_WIDTH), F32),
            pltpu.VMEM((CHUNK, A_WIDTH), F32),
        ],
        compiler_params=pltpu.CompilerParams(
            dimension_semantics=("arbitrary", "arbitrary"),
            vmem_limit_bytes=VMEM_LIMIT),
        name="mixers",
    )(coef, lower_bounds, hgrn_norm.reshape(DEPTH, 1, A_WIDTH),
      pool_scale.reshape(DEPTH, 1, B_WIDTH), proj, fraw, proj, proj, proj, proj, pool_w_b)


def _out_kernel(ya_ref, yb_ref, ga_ref, gb_ref, x_ref, gate_ref, gain_ref,
                wpa_ref, wpb_ref, wo_ref, o_ref):
    pa = _dot(ya_ref[...], wpa_ref[...])
    pb = _dot(yb_ref[...], wpb_ref[...])
    merged = (_sigmoid(ga_ref[...].astype(F32)) * pa
              + _sigmoid(gb_ref[...].astype(F32)) * pb)
    out = _dot(merged.astype(BF16), wo_ref[...])
    inv = lax.rsqrt(jnp.mean(out * out, axis=-1, keepdims=True) + EPS)
    o_ref[...] = x_ref[...] + gate_ref[...] * ((out * inv) * gain_ref[...])


def _out_call(layer, ya, yb, proj, x2, mod5, norm_post, wpa_b, wpb_b, wo_b, tiles_per_batch):
    t = x2.shape[0]
    resident = dict(pipeline_mode=pl.Buffered(1))
    return pl.pallas_call(
        _out_kernel,
        out_shape=jax.ShapeDtypeStruct((t, D_MODEL), F32),
        grid=(t // OUT_TM,),
        in_specs=[
            pl.BlockSpec((OUT_TM, A_WIDTH), lambda i: (i, 0)),
            pl.BlockSpec((OUT_TM, B_WIDTH), lambda i: (i, 0)),
            pl.BlockSpec((OUT_TM, D_MODEL), lambda i: (i, GB_A)),
            pl.BlockSpec((OUT_TM, D_MODEL), lambda i: (i, GB_B)),
            pl.BlockSpec((OUT_TM, D_MODEL), lambda i: (i, 0)),
            pl.BlockSpec((None, None, None, 1, D_MODEL),
                         lambda i: (layer, i // tiles_per_batch, 2, 0, 0)),
            pl.BlockSpec((None, 1, D_MODEL), lambda i: (layer, 0, 0)),
            pl.BlockSpec((None, A_WIDTH, D_MODEL), lambda i: (layer, 0, 0), **resident),
            pl.BlockSpec((None, B_WIDTH, D_MODEL), lambda i: (layer, 0, 0), **resident),
            pl.BlockSpec((None, D_MODEL, D_MODEL), lambda i: (layer, 0, 0), **resident),
        ],
        out_specs=pl.BlockSpec((OUT_TM, D_MODEL), lambda i: (i, 0)),
        compiler_params=pltpu.CompilerParams(
            dimension_semantics=("arbitrary",),
            vmem_limit_bytes=VMEM_LIMIT),
        name="out_stage",
    )(ya, yb, proj, proj, x2, mod5, norm_post.reshape(DEPTH, 1, D_MODEL), wpa_b, wpb_b, wo_b)


def kernel(x, c, w_ada, b_ada, norm_pre, norm_post, w_in, lower_bounds, hgrn_norm,
           pool_w, pool_scale, w_proj_a, w_proj_b, w_out):
    bsz, seq, d = x.shape
    assert d == D_MODEL and seq % IN_TM == 0 and seq % MIX_TC == 0 and seq % OUT_TM == 0
    w_in_b = w_in.astype(BF16)
    wpa_b = w_proj_a.astype(BF16)
    wpb_b = w_proj_b.astype(BF16)
    wo_b = w_out.astype(BF16)
    pool_w_b = pool_w.astype(BF16)

    mod = _ada_call(c, w_ada, b_ada)
    mod5 = mod.reshape(DEPTH, bsz, 3, 1, D_MODEL)
    jj = jnp.arange(DEPTH)
    coef = ((jj[None, :] >= 1) & (jj[None, :] <= jj[:, None])).astype(F32)
    coef = jnp.broadcast_to(coef[:, :, None], (DEPTH, DEPTH, A_WIDTH))

    x2 = x.reshape(bsz * seq, D_MODEL)
    for layer in range(DEPTH):
        proj, fraw = _inproj_call(layer, x2, mod5, norm_pre, w_in_b, seq // IN_TM)
        ya, yb = _mixer_call(layer, proj, fraw, coef, lower_bounds, hgrn_norm, pool_scale,
                             pool_w_b, bsz, seq)
        x2 = _out_call(layer, ya, yb, proj, x2, mod5, norm_post, wpa_b, wpb_b, wo_b,
                       seq // OUT_TM)
    return x2.reshape(bsz, seq, D_MODEL)
```

```python
from functools import partial

import jax
import jax.numpy as jnp
from jax import lax
from jax.experimental import pallas as pl
from jax.experimental.pallas import tpu as pltpu

D_MODEL = 2048
DEPTH = 4
CHUNK = 64
EPS = 1e-6
MIN_FORGET = 1e-30
HEAD_DIM = 128
A_WIDTH = D_MODEL // 2
N_HEADS = A_WIDTH // HEAD_DIM
B_WIDTH = D_MODEL // 2
POOL_WINDOWS = (2, 4, 8, 16)
POOL_GW = B_WIDTH // len(POOL_WINDOWS)
POOL_HALO = 16
IN_COLS = 6 * A_WIDTH + 2 * D_MODEL
COL_BLK = 1024
N_COL_BLK = IN_COLS // COL_BLK
N_GATE_BLK = 2 * D_MODEL // COL_BLK
FIRST_GATE_BLK = 6 * A_WIDTH // COL_BLK
SLOT_Q, SLOT_VA, SLOT_ZA, SLOT_ZB = 0, 1, 2, 3
SLOT_F, SLOT_VB = 0, 1

SAFE_CHUNK_LOG_DECAY = 120.0

TM = 1024
CHUNKS_PER_STEP = TM // CHUNK // N_GATE_BLK
POOL_ROWS = TM // N_GATE_BLK
OUT_TM = 512
NORM_TM = 1024
VMEM_LIMIT = 60 * 1024 * 1024

F32 = jnp.float32
BF16 = jnp.bfloat16


def _sigmoid(v):
    return 1.0 / (1.0 + jnp.exp(-v))


def _dot(a, b):
    return jnp.dot(a, b, preferred_element_type=F32)


def _dot_nt(a, b):
    return lax.dot_general(a, b, (((1,), (1,)), ((), ())), preferred_element_type=F32)


def _dot_tn(a, b):
    return lax.dot_general(a, b, (((0,), (0,)), ((), ())), preferred_element_type=F32)


def _prenorm(x, gain, shift, scale):
    inv = lax.rsqrt(jnp.mean(x * x, axis=-1, keepdims=True) + EPS)
    return ((x * inv) * gain) * (1.0 + scale) + shift


def _ada_kernel(ct_ref, w_ref, b_ref, o_ref):
    ct = ct_ref[...]
    ca = ct * _sigmoid(ct)
    w = w_ref[...]
    rows = [jnp.sum(w * ca[:, b:b + 1], axis=0, keepdims=True)
            for b in range(ct.shape[1])]
    o_ref[...] = jnp.concatenate(rows, axis=0) + b_ref[...]


def _ada_call(c, w_ada, b_ada):
    bsz = c.shape[0]
    tn = 1024
    n3 = 3 * D_MODEL
    return pl.pallas_call(
        _ada_kernel,
        out_shape=jax.ShapeDtypeStruct((DEPTH, bsz, n3), F32),
        grid=(DEPTH, n3 // tn),
        in_specs=[
            pl.BlockSpec((D_MODEL, bsz), lambda l, j: (0, 0)),
            pl.BlockSpec((None, D_MODEL, tn), lambda l, j: (l, 0, j)),
            pl.BlockSpec((None, 1, tn), lambda l, j: (l, 0, j)),
        ],
        out_specs=pl.BlockSpec((None, bsz, tn), lambda l, j: (l, 0, j)),
        compiler_params=pltpu.CompilerParams(
            dimension_semantics=("arbitrary", "arbitrary"),
            vmem_limit_bytes=VMEM_LIMIT),
        name="ada_mod",
    )(c.T, w_ada, b_ada.reshape(DEPTH, 1, n3))


def _norm_kernel(x_ref, gain_ref, shift_ref, scale_ref, h_ref):
    h_ref[...] = _prenorm(x_ref[...], gain_ref[...], shift_ref[...], scale_ref[...]).astype(BF16)


def _mod_spec(layer, which, tiles_per_batch):
    return pl.BlockSpec((None, None, None, 1, D_MODEL),
                        lambda i: (layer, i // tiles_per_batch, which, 0, 0))


def _norm_call(x2, mod5, norm_pre, seq):
    t = x2.shape[0]
    tpb = seq // NORM_TM
    return pl.pallas_call(
        _norm_kernel,
        out_shape=jax.ShapeDtypeStruct((t, D_MODEL), BF16),
        grid=(t // NORM_TM,),
        in_specs=[
            pl.BlockSpec((NORM_TM, D_MODEL), lambda i: (i, 0)),
            pl.BlockSpec((None, 1, D_MODEL), lambda i: (0, 0, 0)),
            _mod_spec(0, 0, tpb), _mod_spec(0, 1, tpb),
        ],
        out_specs=pl.BlockSpec((NORM_TM, D_MODEL), lambda i: (i, 0)),
        compiler_params=pltpu.CompilerParams(
            dimension_semantics=("arbitrary",), vmem_limit_bytes=VMEM_LIMIT),
        name="pre_norm",
    )(x2, norm_pre.reshape(DEPTH, 1, D_MODEL), mod5, mod5)


def _split3(v):
    hi = v.astype(BF16)
    r1 = v - hi.astype(F32)
    mid = r1.astype(BF16)
    lo = (r1 - mid.astype(F32)).astype(BF16)
    return hi, mid, lo


def _hgrn_chunk(row0, lb, one_m_lb, hn, tri, causal, col, pb_ref, pf_ref, ya_ref, st_ref,
                direct_refs=None):
    rows = pl.ds(row0, CHUNK)
    a = pf_ref[SLOT_F, pl.ds(row0 + POOL_HALO, CHUNK), :]
    sig = _sigmoid(a)
    f = lb + one_m_lb * sig
    lf = jnp.log(jnp.maximum(f, MIN_FORGET))
    kk = one_m_lb * (1.0 - sig)
    qr = pb_ref[SLOT_Q, rows, :].astype(F32)
    q = qr * _sigmoid(qr)

    hi, mid, lo = _split3(lf)
    b = _dot(tri, hi) + _dot(tri, mid) + _dot(tri, lo)
    b_last = b[CHUNK - 1:CHUNK, :]
    q_st = (q * jnp.exp(b)).astype(BF16)
    k_st = (kk * jnp.exp(b_last - b)).astype(BF16)
    st_decay = jnp.exp(b_last)

    if direct_refs is None:
        half = 0.5 * b_last
        q_in = (q * jnp.exp(b - half)).astype(BF16)
        k_in = (kk * jnp.exp(half - b)).astype(BF16)
    else:
        a_ref, bs_ref, qs_ref, ks_ref = direct_refs
        bs_ref[...] = b
        qs_ref[...] = q
        ks_ref[...] = kk
        a_ref[...] = jnp.zeros_like(a_ref)

        def src_body(s, c2):
            brow = bs_ref[pl.ds(s, 1), :]
            krow = ks_ref[pl.ds(s, 1), :]
            p = jnp.exp(jnp.minimum(bs_ref[...] - brow, 0.0)) * qs_ref[...] * krow
            for h in range(N_HEADS):
                cs = slice(h * HEAD_DIM, (h + 1) * HEAD_DIM)
                colv = jnp.sum(p[:, cs], axis=1, keepdims=True)
                a_ref[h] = jnp.where(col == s, colv, a_ref[h])
            return c2

        lax.fori_loop(0, CHUNK, src_body, 0)

    for h in range(N_HEADS):
        cs = slice(h * HEAD_DIM, (h + 1) * HEAD_DIM)
        if direct_refs is None:
            sc = _dot_nt(q_in[:, cs], k_in[:, cs])
        else:
            sc = direct_refs[0][h]
        sc = jnp.where(causal, sc, 0.0).astype(BF16)
        v = pb_ref[SLOT_VA, rows, cs]
        st = st_ref[h]
        o = _dot_nt(q_st[:, cs], st.astype(BF16)) + _dot(sc, v)
        st_ref[h] = st * st_decay[:, cs] + _dot_tn(v, k_st[:, cs])
        o = o * lax.rsqrt(jnp.mean(o * o, axis=-1, keepdims=True) + EPS)
        z = pb_ref[SLOT_ZA, rows, cs].astype(F32)
        ya_ref[rows, cs] = (o * hn[:, cs] * (z * _sigmoid(z))).astype(BF16)
    return b_last


def _pool_rows(row0, seq_row0, pb_ref, pf_ref, pw_ref, pscale_ref, yb_ref):
    n = POOL_ROWS + POOL_HALO
    pos = (seq_row0 + row0 + lax.broadcasted_iota(jnp.int32, (POOL_ROWS, POOL_GW), 0) + 1
           ).astype(F32)
    rows = pl.ds(row0, POOL_ROWS)
    for g, w in enumerate(POOL_WINDOWS):
        cs = slice(g * POOL_GW, (g + 1) * POOL_GW)
        ext = pf_ref[SLOT_VB, pl.ds(row0, n), cs]
        tot = ext
        d = 1
        while d < w:
            tot = tot + pltpu.roll(tot, d, axis=0)
            d *= 2
        cur = ext[POOL_HALO:, :]
        pooled = tot[POOL_HALO:, :] / jnp.minimum(pos, float(w)) - cur
        mixed = _dot(pooled.astype(BF16), pw_ref[g]) * pscale_ref[:, cs]
        z = pb_ref[SLOT_ZB, rows, cs].astype(F32)
        yb_ref[rows, cs] = (mixed * (z * _sigmoid(z))).astype(BF16)


def _proj_mix_kernel(coef_ref, lbraw_ref, hn_ref, pscale_ref, pw_ref, h_ref, w_ref,
                     g_ref, ya_ref, yb_ref,
                     pb_ref, pf_ref, st_ref, snap_ref, minb_ref, a_ref, bs_ref, qs_ref, ks_ref):
    i = pl.program_id(1)
    j = pl.program_id(2)
    nt = pl.num_programs(1) - 1
    has_mm = i < nt
    has_mix = i > 0
    is_gate = j < N_GATE_BLK

    def lower_bound():
        lbr = lbraw_ref[...]
        e = jnp.exp(lbr - jnp.max(lbr, axis=0, keepdims=True))
        sm = e / jnp.sum(e, axis=0, keepdims=True)
        return jnp.sum(coef_ref[...] * sm, axis=0, keepdims=True)

    def masks():
        row = lax.broadcasted_iota(jnp.int32, (CHUNK, CHUNK), 0)
        col = lax.broadcasted_iota(jnp.int32, (CHUNK, CHUNK), 1)
        causal = row >= col
        return causal.astype(BF16), causal, col

    def mix_step():
        lb = lower_bound()
        tri, causal, col = masks()
        hn = hn_ref[...]
        minb = minb_ref[...]
        for cc in range(CHUNKS_PER_STEP):
            row0 = pl.multiple_of((j * CHUNKS_PER_STEP + cc) * CHUNK, CHUNK)
            b_last = _hgrn_chunk(row0, lb, 1.0 - lb, hn, tri, causal, col,
                                 pb_ref, pf_ref, ya_ref, st_ref)
            minb = jnp.minimum(minb, b_last)
        minb_ref[...] = minb
        _pool_rows(pl.multiple_of(j * POOL_ROWS, POOL_ROWS), (i - 1) * TM,
                   pb_ref, pf_ref, pw_ref, pscale_ref, yb_ref)

    @pl.when(has_mix & (j == 0))
    def _():
        @pl.when(i == 1)
        def _():
            st_ref[...] = jnp.zeros_like(st_ref)
            pf_ref[SLOT_VB, 0:POOL_HALO, :] = jnp.zeros((POOL_HALO, COL_BLK), F32)

        snap_ref[...] = st_ref[...]
        minb_ref[...] = jnp.zeros_like(minb_ref)

    @pl.when(is_gate & has_mm & has_mix)
    def _():
        g_ref[...] = _dot(h_ref[...], w_ref[...]).astype(BF16)
        mix_step()

    @pl.when(is_gate & has_mm & jnp.logical_not(has_mix))
    def _():
        g_ref[...] = _dot(h_ref[...], w_ref[...]).astype(BF16)

    @pl.when(is_gate & jnp.logical_not(has_mm))
    def _():
        mix_step()

    is_f32_blk = (j == N_GATE_BLK + 1) | (j == N_GATE_BLK + 4)

    @pl.when(jnp.logical_not(is_gate) & has_mm & jnp.logical_not(is_f32_blk))
    def _():
        jj = j - N_GATE_BLK
        slot = jnp.where(jj == 0, SLOT_Q, jnp.where(jj == 5, SLOT_ZB, jj - 1))
        pb_ref[slot] = _dot(h_ref[...], w_ref[...]).astype(BF16)

    @pl.when(jnp.logical_not(is_gate) & has_mm & is_f32_blk)
    def _():
        slot = jnp.where(j == N_GATE_BLK + 1, SLOT_F, SLOT_VB)
        pf_ref[slot, POOL_HALO:POOL_HALO + TM, :] = _dot(h_ref[...], w_ref[...])

    @pl.when(has_mix & (j == N_GATE_BLK - 1))
    def _():
        pf_ref[SLOT_VB, 0:POOL_HALO, :] = pf_ref[SLOT_VB, TM:TM + POOL_HALO, :]

        @pl.when(jnp.min(minb_ref[...]) < -SAFE_CHUNK_LOG_DECAY)
        def _():
            st_ref[...] = snap_ref[...]
            lb = lower_bound()
            tri, causal, col = masks()
            hn = hn_ref[...]

            def redo(ci, carry):
                _hgrn_chunk(pl.multiple_of(ci * CHUNK, CHUNK), lb, 1.0 - lb, hn, tri, causal,
                            col, pb_ref, pf_ref, ya_ref, st_ref,
                            direct_refs=(a_ref, bs_ref, qs_ref, ks_ref))
                return carry

            lax.fori_loop(0, TM // CHUNK, redo, 0)


def _proj_mix_call(layer, h, coef, lower_bounds, hgrn_norm, pool_scale, pool_w_b, w_in_b,
                   bsz, seq):
    t = h.shape[0]
    nt = seq // TM

    def h_map(b, i, j):
        return (b * nt + jnp.minimum(i, nt - 1), 0)

    def w_map(b, i, j):
        cb = (j + FIRST_GATE_BLK) % N_COL_BLK
        return (layer, 0, jnp.where(i == nt, FIRST_GATE_BLK - 1, cb))

    def g_map(b, i, j):
        gb = jnp.minimum(j, N_GATE_BLK - 1)
        return (b * nt + jnp.minimum(i, nt - 1), jnp.where(i == nt, N_GATE_BLK - 1, gb))

    def y_map(b, i, j):
        return (b * nt + jnp.maximum(i - 1, 0), 0)

    const3 = lambda b, i, j: (layer, 0, 0)
    return pl.pallas_call(
        _proj_mix_kernel,
        out_shape=(jax.ShapeDtypeStruct((t, 2 * D_MODEL), BF16),
                   jax.ShapeDtypeStruct((t, A_WIDTH), BF16),
                   jax.ShapeDtypeStruct((t, B_WIDTH), BF16)),
        grid=(bsz, nt + 1, N_COL_BLK),
        in_specs=[
            pl.BlockSpec((None, DEPTH, A_WIDTH), const3),
            pl.BlockSpec((DEPTH, A_WIDTH), lambda b, i, j: (0, 0)),
            pl.BlockSpec((None, 1, A_WIDTH), const3),
            pl.BlockSpec((None, 1, B_WIDTH), const3),
            pl.BlockSpec((None, len(POOL_WINDOWS), POOL_GW, POOL_GW),
                         lambda b, i, j: (layer, 0, 0, 0)),
            pl.BlockSpec((TM, D_MODEL), h_map),
            pl.BlockSpec((None, D_MODEL, COL_BLK), w_map),
        ],
        out_specs=(pl.BlockSpec((TM, COL_BLK), g_map),
                   pl.BlockSpec((TM, A_WIDTH), y_map),
                   pl.BlockSpec((TM, B_WIDTH), y_map)),
        scratch_shapes=[
            pltpu.VMEM((4, TM, COL_BLK), BF16),
            pltpu.VMEM((2, TM + POOL_HALO, COL_BLK), F32),
            pltpu.VMEM((N_HEADS, HEAD_DIM, HEAD_DIM), F32),
            pltpu.VMEM((N_HEADS, HEAD_DIM, HEAD_DIM), F32),
            pltpu.VMEM((1, A_WIDTH), F32),
            pltpu.VMEM((N_HEADS, CHUNK, CHUNK), F32),
            pltpu.VMEM((CHUNK, A_WIDTH), F32),
            pltpu.VMEM((CHUNK, A_WIDTH), F32),
            pltpu.VMEM((CHUNK, A_WIDTH), F32),
        ],
        compiler_params=pltpu.CompilerParams(
            dimension_semantics=("arbitrary", "arbitrary", "arbitrary"),
            vmem_limit_bytes=VMEM_LIMIT),
        name="proj_mix",
    )(coef, lower_bounds, hgrn_norm.reshape(DEPTH, 1, A_WIDTH),
      pool_scale.reshape(DEPTH, 1, B_WIDTH), pool_w_b, h, w_in_b)


def _out_kernel(emit_h, ya_ref, yb_ref, ga_ref, gb_ref, x_ref, gate_ref, gain_ref,
                wpa_ref, wpb_ref, wo_ref, ngain_ref, nshift_ref, nscale_ref, o_ref, h_ref=None):
    pa = _dot(ya_ref[...], wpa_ref[...])
    pb = _dot(yb_ref[...], wpb_ref[...])
    merged = (_sigmoid(ga_ref[...].astype(F32)) * pa
              + _sigmoid(gb_ref[...].astype(F32)) * pb)
    out = _dot(merged.astype(BF16), wo_ref[...])
    inv = lax.rsqrt(jnp.mean(out * out, axis=-1, keepdims=True) + EPS)
    xn = x_ref[...] + gate_ref[...] * ((out * inv) * gain_ref[...])
    o_ref[...] = xn
    if emit_h:
        h_ref[...] = _prenorm(xn, ngain_ref[...], nshift_ref[...], nscale_ref[...]).astype(BF16)


def _out_call(layer, ya, yb, g, x2, mod5, norm_post, norm_pre, wpa_b, wpb_b, wo_b, seq):
    t = x2.shape[0]
    tpb = seq // OUT_TM
    emit_h = layer + 1 < DEPTH
    nxt = min(layer + 1, DEPTH - 1)
    resident = dict(pipeline_mode=pl.Buffered(1))
    row_spec = lambda width, cb=0: pl.BlockSpec((OUT_TM, width), lambda i: (i, cb))
    out_shape = [jax.ShapeDtypeStruct((t, D_MODEL), F32)]
    out_specs = [row_spec(D_MODEL)]
    if emit_h:
        out_shape.append(jax.ShapeDtypeStruct((t, D_MODEL), BF16))
        out_specs.append(row_spec(D_MODEL))
    res = pl.pallas_call(
        partial(_out_kernel, emit_h),
        out_shape=tuple(out_shape),
        grid=(t // OUT_TM,),
        in_specs=[
            row_spec(A_WIDTH), row_spec(B_WIDTH),
            row_spec(D_MODEL, 0), row_spec(D_MODEL, 1),
            row_spec(D_MODEL),
            _mod_spec(layer, 2, tpb),
            pl.BlockSpec((None, 1, D_MODEL), lambda i: (layer, 0, 0)),
            pl.BlockSpec((None, A_WIDTH, D_MODEL), lambda i: (layer, 0, 0), **resident),
            pl.BlockSpec((None, B_WIDTH, D_MODEL), lambda i: (layer, 0, 0), **resident),
            pl.BlockSpec((None, D_MODEL, D_MODEL), lambda i: (layer, 0, 0), **resident),
            pl.BlockSpec((None, 1, D_MODEL), lambda i: (nxt, 0, 0)),
            _mod_spec(nxt, 0, tpb), _mod_spec(nxt, 1, tpb),
        ],
        out_specs=tuple(out_specs),
        compiler_params=pltpu.CompilerParams(
            dimension_semantics=("arbitrary",),
            vmem_limit_bytes=VMEM_LIMIT),
        name="out_stage",
    )(ya, yb, g, g, x2, mod5, norm_post.reshape(DEPTH, 1, D_MODEL), wpa_b, wpb_b, wo_b,
      norm_pre.reshape(DEPTH, 1, D_MODEL), mod5, mod5)
    return res if emit_h else (res[0], None)


def kernel(x, c, w_ada, b_ada, norm_pre, norm_post, w_in, lower_bounds, hgrn_norm,
           pool_w, pool_scale, w_proj_a, w_proj_b, w_out):
    bsz, seq, d = x.shape
    assert d == D_MODEL and seq % TM == 0 and seq % OUT_TM == 0 and seq % NORM_TM == 0
    w_in_b = w_in.astype(BF16)
    wpa_b = w_proj_a.astype(BF16)
    wpb_b = w_proj_b.astype(BF16)
    wo_b = w_out.astype(BF16)
    pool_w_b = pool_w.astype(BF16)

    mod = _ada_call(c, w_ada, b_ada)
    mod5 = mod.reshape(DEPTH, bsz, 3, 1, D_MODEL)
    jj = jnp.arange(DEPTH)
    coef = ((jj[None, :] >= 1) & (jj[None, :] <= jj[:, None])).astype(F32)
    coef = jnp.broadcast_to(coef[:, :, None], (DEPTH, DEPTH, A_WIDTH))

    x2 = x.reshape(bsz * seq, D_MODEL)
    h = _norm_call(x2, mod5, norm_pre, seq)
    for layer in range(DEPTH):
        g, ya, yb = _proj_mix_call(layer, h, coef, lower_bounds, hgrn_norm, pool_scale,
                                   pool_w_b, w_in_b, bsz, seq)
        x2, h = _out_call(layer, ya, yb, g, x2, mod5, norm_post, norm_pre,
                          wpa_b, wpb_b, wo_b, seq)
    return x2.reshape(bsz, seq, D_MODEL)
```

```python
from functools import partial

import jax
import jax.numpy as jnp
from jax import lax
from jax.experimental import pallas as pl
from jax.experimental.pallas import tpu as pltpu

D_MODEL = 2048
DEPTH = 4
CHUNK = 64
EPS = 1e-6
MIN_FORGET = 1e-30
HEAD_DIM = 128
A_WIDTH = D_MODEL // 2
N_HEADS = A_WIDTH // HEAD_DIM
B_WIDTH = D_MODEL // 2
POOL_WINDOWS = (2, 4, 8, 16)
POOL_GW = B_WIDTH // len(POOL_WINDOWS)
POOL_HALO = 16
IN_COLS = 6 * A_WIDTH + 2 * D_MODEL
COL_BLK = 1024
N_COL_BLK = IN_COLS // COL_BLK
N_GATE_BLK = 2 * D_MODEL // COL_BLK
FIRST_GATE_BLK = 6 * A_WIDTH // COL_BLK
SLOT_Q, SLOT_VA, SLOT_ZA, SLOT_ZB = 0, 1, 2, 3
SLOT_F, SLOT_VB = 0, 1

SAFE_CHUNK_LOG_DECAY = 120.0

TM = 1024
CHUNKS_PER_STEP = TM // CHUNK // N_GATE_BLK
POOL_ROWS = TM // N_GATE_BLK
OUT_TM = 512
NORM_TM = 1024
VMEM_LIMIT = 60 * 1024 * 1024

F32 = jnp.float32
BF16 = jnp.bfloat16


LOG2_E = 1.4426950408889634


def _sigmoid(v):
    return 0.5 * jnp.tanh(0.5 * v) + 0.5


def _silu(v):
    hv = 0.5 * v
    return hv * jnp.tanh(hv) + hv


def _dot(a, b):
    return jnp.dot(a, b, preferred_element_type=F32)


def _dot_nt(a, b):
    return lax.dot_general(a, b, (((1,), (1,)), ((), ())), preferred_element_type=F32)


def _dot_tn(a, b):
    return lax.dot_general(a, b, (((0,), (0,)), ((), ())), preferred_element_type=F32)


def _prenorm(x, gain, shift, scale):
    inv = lax.rsqrt(jnp.mean(x * x, axis=-1, keepdims=True) + EPS)
    return ((x * inv) * gain) * (1.0 + scale) + shift


def _ada_kernel(ct_ref, w_ref, b_ref, o_ref):
    ct = ct_ref[...]
    ca = _silu(ct)
    w = w_ref[...]
    rows = [jnp.sum(w * ca[:, b:b + 1], axis=0, keepdims=True)
            for b in range(ct.shape[1])]
    o_ref[...] = jnp.concatenate(rows, axis=0) + b_ref[...]


def _ada_call(c, w_ada, b_ada):
    bsz = c.shape[0]
    tn = 1024
    n3 = 3 * D_MODEL
    return pl.pallas_call(
        _ada_kernel,
        out_shape=jax.ShapeDtypeStruct((DEPTH, bsz, n3), F32),
        grid=(DEPTH, n3 // tn),
        in_specs=[
            pl.BlockSpec((D_MODEL, bsz), lambda l, j: (0, 0)),
            pl.BlockSpec((None, D_MODEL, tn), lambda l, j: (l, 0, j)),
            pl.BlockSpec((None, 1, tn), lambda l, j: (l, 0, j)),
        ],
        out_specs=pl.BlockSpec((None, bsz, tn), lambda l, j: (l, 0, j)),
        compiler_params=pltpu.CompilerParams(
            dimension_semantics=("arbitrary", "arbitrary"),
            vmem_limit_bytes=VMEM_LIMIT),
        name="ada_mod",
    )(c.T, w_ada, b_ada.reshape(DEPTH, 1, n3))


def _norm_kernel(x_ref, gain_ref, shift_ref, scale_ref, h_ref):
    h_ref[...] = _prenorm(x_ref[...], gain_ref[...], shift_ref[...], scale_ref[...]).astype(BF16)


def _mod_spec(layer, which, tiles_per_batch):
    return pl.BlockSpec((None, None, None, 1, D_MODEL),
                        lambda i: (layer, i // tiles_per_batch, which, 0, 0))


def _norm_call(x2, mod5, norm_pre, seq):
    t = x2.shape[0]
    tpb = seq // NORM_TM
    return pl.pallas_call(
        _norm_kernel,
        out_shape=jax.ShapeDtypeStruct((t, D_MODEL), BF16),
        grid=(t // NORM_TM,),
        in_specs=[
            pl.BlockSpec((NORM_TM, D_MODEL), lambda i: (i, 0)),
            pl.BlockSpec((None, 1, D_MODEL), lambda i: (0, 0, 0)),
            _mod_spec(0, 0, tpb), _mod_spec(0, 1, tpb),
        ],
        out_specs=pl.BlockSpec((NORM_TM, D_MODEL), lambda i: (i, 0)),
        compiler_params=pltpu.CompilerParams(
            dimension_semantics=("arbitrary",), vmem_limit_bytes=VMEM_LIMIT),
        name="pre_norm",
    )(x2, norm_pre.reshape(DEPTH, 1, D_MODEL), mod5, mod5)


def _split3(v):
    hi = v.astype(BF16)
    r1 = v - hi.astype(F32)
    mid = r1.astype(BF16)
    lo = (r1 - mid.astype(F32)).astype(BF16)
    return hi, mid, lo


def _hgrn_chunk_stages(row0, lb, one_m_lb, hn, tri, causal, col, pb_ref, pf_ref, ya_ref,
                       st_ref, b_last_out, direct_refs=None):
    rows = pl.ds(row0, CHUNK)
    a = pf_ref[SLOT_F, pl.ds(row0 + POOL_HALO, CHUNK), :]
    half_gap = 0.5 * one_m_lb
    gt = half_gap * jnp.tanh(0.5 * a)
    f = (lb + half_gap) + gt
    kk = half_gap - gt
    lf = jnp.log(jnp.maximum(f, MIN_FORGET))
    q = _silu(pb_ref[SLOT_Q, rows, :].astype(F32))
    yield

    hi, mid, lo = _split3(lf)
    yield

    b = _dot(tri, jnp.concatenate([hi, mid, lo], axis=0))
    yield

    b_last = b[CHUNK - 1:CHUNK, :]
    b_last_out.append(b_last)
    st_decay = jnp.exp(b_last)
    if direct_refs is None:
        half = 0.5 * b_last
        d2 = (b - half) * LOG2_E
        q_in = q * jnp.exp2(d2)
        k_in = kk * jnp.exp2(-d2)
        e_half = jnp.exp(half)
        q_st = (q_in * e_half).astype(BF16)
        k_st = (k_in * e_half).astype(BF16)
        q_in = q_in.astype(BF16)
        k_in = k_in.astype(BF16)
    else:
        q_st = (q * jnp.exp(b)).astype(BF16)
        k_st = (kk * jnp.exp(b_last - b)).astype(BF16)
    yield

    heads = [slice(h * HEAD_DIM, (h + 1) * HEAD_DIM) for h in range(N_HEADS)]
    if direct_refs is None:
        scores = [_dot_nt(q_in[:, cs], k_in[:, cs]) for cs in heads]
    else:
        a_ref, bs_ref, qs_ref, ks_ref = direct_refs
        bs_ref[...] = b
        qs_ref[...] = q
        ks_ref[...] = kk
        a_ref[...] = jnp.zeros_like(a_ref)

        def src_body(s, c2):
            brow = bs_ref[pl.ds(s, 1), :]
            krow = ks_ref[pl.ds(s, 1), :]
            p = jnp.exp(jnp.minimum(bs_ref[...] - brow, 0.0)) * qs_ref[...] * krow
            for h, cs in enumerate(heads):
                colv = jnp.sum(p[:, cs], axis=1, keepdims=True)
                a_ref[h] = jnp.where(col == s, colv, a_ref[h])
            return c2

        lax.fori_loop(0, CHUNK, src_body, 0)
        scores = [a_ref[h] for h in range(N_HEADS)]
    yield

    scores = [jnp.where(causal, sc, 0.0).astype(BF16) for sc in scores]
    yield

    outs, updates = [], []
    for h, cs in enumerate(heads):
        v = pb_ref[SLOT_VA, rows, cs]
        outs.append(_dot_nt(q_st[:, cs], st_ref[h].astype(BF16)) + _dot(scores[h], v))
        updates.append(_dot_tn(v, k_st[:, cs]))
    yield

    for h, cs in enumerate(heads):
        if h == N_HEADS // 2:
            yield
        st_ref[h] = st_ref[h] * st_decay[:, cs] + updates[h]
        o = outs[h]
        o = o * lax.rsqrt(jnp.mean(o * o, axis=-1, keepdims=True) + EPS)
        z = pb_ref[SLOT_ZA, rows, cs].astype(F32)
        ya_ref[rows, cs] = (o * hn[:, cs] * _silu(z)).astype(BF16)


def _pool_group_stages(g, row0, seq_row0, pb_ref, pf_ref, pw_ref, pscale_ref, yb_ref):
    w = POOL_WINDOWS[g]
    n = POOL_ROWS + POOL_HALO
    pos = (seq_row0 + row0 + lax.broadcasted_iota(jnp.int32, (POOL_ROWS, POOL_GW), 0) + 1
           ).astype(F32)
    rows = pl.ds(row0, POOL_ROWS)
    cs = slice(g * POOL_GW, (g + 1) * POOL_GW)
    ext = pf_ref[SLOT_VB, pl.ds(row0, n), cs]
    tot = ext
    d = 1
    while d < w:
        tot = tot + pltpu.roll(tot, d, axis=0)
        d *= 2
    cur = ext[POOL_HALO:, :]
    pooled = (tot[POOL_HALO:, :] / jnp.minimum(pos, float(w)) - cur).astype(BF16)
    yield

    mixed = _dot(pooled, pw_ref[g])
    yield

    z = pb_ref[SLOT_ZB, rows, cs].astype(F32)
    yb_ref[rows, cs] = (mixed * pscale_ref[:, cs] * _silu(z)).astype(BF16)


FUSED_SLOTS = (
    (0, 0),
    (0, 1, 1),
    (1, 0, 'p0'),
    (0, 'p0', 1, 2, 2),
    (1, 2, 0, 'p0', 'p1'),
    (0, 'p1', 1, 2, 3),
    (2, 0, 0, 3, 'p1'),
    (1, 3, 2, 'p2'),
    ('p2', 1, 1, 3),
    (2, 3, 'p2', 'p3'),
    ('p3', 2, 2, 3),
    (3, 'p3'),
    (3, 3),
)
MM_ROW_SPLIT = 4
MM_COL_SPLIT = 4


def _proj_mix_kernel(coef_ref, lbraw_ref, hn_ref, pscale_ref, pw_ref, h_ref, w_ref,
                     g_ref, ya_ref, yb_ref,
                     pb_ref, pf_ref, st_ref, snap_ref, minb_ref, lb_ref,
                     a_ref, bs_ref, qs_ref, ks_ref):
    i = pl.program_id(1)
    j = pl.program_id(2)
    nt = pl.num_programs(1) - 1
    has_mm = i < nt
    has_mix = i > 0
    is_gate = j < N_GATE_BLK

    def lower_bound():
        lbr = lbraw_ref[...]
        e = jnp.exp(lbr - jnp.max(lbr, axis=0, keepdims=True))
        sm = e / jnp.sum(e, axis=0, keepdims=True)
        return jnp.sum(coef_ref[...] * sm, axis=0, keepdims=True)

    def masks():
        row = lax.broadcasted_iota(jnp.int32, (CHUNK, CHUNK), 0)
        col = lax.broadcasted_iota(jnp.int32, (CHUNK, CHUNK), 1)
        causal = row >= col
        tri = causal.astype(BF16)
        return jnp.concatenate([tri, tri, tri], axis=1), causal, col

    def mm_slices():
        mb, nb = TM // MM_ROW_SPLIT, COL_BLK // MM_COL_SPLIT
        for c in range(MM_COL_SPLIT):
            for r in range(MM_ROW_SPLIT):
                rs, cs = slice(r * mb, (r + 1) * mb), slice(c * nb, (c + 1) * nb)
                g_ref[rs, cs] = _dot(h_ref[rs, :], w_ref[:, cs]).astype(BF16)
                yield

    def mix_step(with_mm):
        lb = lb_ref[...]
        tri, causal, col = masks()
        hn = hn_ref[...]
        b_lasts = []
        chunks = [
            _hgrn_chunk_stages(pl.multiple_of((j * CHUNKS_PER_STEP + cc) * CHUNK, CHUNK),
                               lb, 1.0 - lb, hn, tri, causal, col, pb_ref, pf_ref, ya_ref,
                               st_ref, b_lasts)
            for cc in range(CHUNKS_PER_STEP)]
        pools = {
            'p%d' % g: _pool_group_stages(g, pl.multiple_of(j * POOL_ROWS, POOL_ROWS),
                                          (i - 1) * TM, pb_ref, pf_ref, pw_ref, pscale_ref,
                                          yb_ref)
            for g in range(len(POOL_WINDOWS))}
        slices = mm_slices() if with_mm else iter(())
        for slot in FUSED_SLOTS:
            for item in slot:
                next(pools[item] if isinstance(item, str) else chunks[item], None)
            next(slices, None)
        for _ in slices:
            pass
        done = object()
        assert all(next(gen, done) is done for gen in [*chunks, *pools.values()])
        minb = minb_ref[...]
        for b_last in b_lasts:
            minb = jnp.minimum(minb, b_last)
        minb_ref[...] = minb

    @pl.when(has_mix & (j == 0))
    def _():
        @pl.when(i == 1)
        def _():
            st_ref[...] = jnp.zeros_like(st_ref)
            pf_ref[SLOT_VB, 0:POOL_HALO, :] = jnp.zeros((POOL_HALO, COL_BLK), F32)

        snap_ref[...] = st_ref[...]
        minb_ref[...] = jnp.zeros_like(minb_ref)
        lb_ref[...] = lower_bound()

    @pl.when(is_gate & has_mm & has_mix)
    def _():
        mix_step(with_mm=True)

    @pl.when(is_gate & has_mm & jnp.logical_not(has_mix))
    def _():
        g_ref[...] = _dot(h_ref[...], w_ref[...]).astype(BF16)

    @pl.when(is_gate & jnp.logical_not(has_mm))
    def _():
        mix_step(with_mm=False)

    is_f32_blk = (j == N_GATE_BLK + 1) | (j == N_GATE_BLK + 4)

    @pl.when(jnp.logical_not(is_gate) & has_mm & jnp.logical_not(is_f32_blk))
    def _():
        jj = j - N_GATE_BLK
        slot = jnp.where(jj == 0, SLOT_Q, jnp.where(jj == 5, SLOT_ZB, jj - 1))
        pb_ref[slot] = _dot(h_ref[...], w_ref[...]).astype(BF16)

    @pl.when(jnp.logical_not(is_gate) & has_mm & is_f32_blk)
    def _():
        slot = jnp.where(j == N_GATE_BLK + 1, SLOT_F, SLOT_VB)
        pf_ref[slot, POOL_HALO:POOL_HALO + TM, :] = _dot(h_ref[...], w_ref[...])

    @pl.when(has_mix & (j == N_GATE_BLK - 1))
    def _():
        pf_ref[SLOT_VB, 0:POOL_HALO, :] = pf_ref[SLOT_VB, TM:TM + POOL_HALO, :]

        @pl.when(jnp.min(minb_ref[...]) < -SAFE_CHUNK_LOG_DECAY)
        def _():
            st_ref[...] = snap_ref[...]
            lb = lb_ref[...]
            tri, causal, col = masks()
            hn = hn_ref[...]

            def redo(ci, carry):
                for _ in _hgrn_chunk_stages(
                        pl.multiple_of(ci * CHUNK, CHUNK), lb, 1.0 - lb, hn, tri, causal, col,
                        pb_ref, pf_ref, ya_ref, st_ref, [],
                        direct_refs=(a_ref, bs_ref, qs_ref, ks_ref)):
                    pass
                return carry

            lax.fori_loop(0, TM // CHUNK, redo, 0)


def _proj_mix_call(layer, h, coef, lower_bounds, hgrn_norm, pool_scale, pool_w_b, w_in_b,
                   bsz, seq):
    t = h.shape[0]
    nt = seq // TM

    def h_map(b, i, j):
        return (b * nt + jnp.minimum(i, nt - 1), 0)

    def w_map(b, i, j):
        cb = (j + FIRST_GATE_BLK) % N_COL_BLK
        return (layer, 0, jnp.where(i == nt, FIRST_GATE_BLK - 1, cb))

    def g_map(b, i, j):
        gb = jnp.minimum(j, N_GATE_BLK - 1)
        return (b * nt + jnp.minimum(i, nt - 1), jnp.where(i == nt, N_GATE_BLK - 1, gb))

    def y_map(b, i, j):
        return (b * nt + jnp.maximum(i - 1, 0), 0)

    const3 = lambda b, i, j: (layer, 0, 0)
    return pl.pallas_call(
        _proj_mix_kernel,
        out_shape=(jax.ShapeDtypeStruct((t, 2 * D_MODEL), BF16),
                   jax.ShapeDtypeStruct((t, A_WIDTH), BF16),
                   jax.ShapeDtypeStruct((t, B_WIDTH), BF16)),
        grid=(bsz, nt + 1, N_COL_BLK),
        in_specs=[
            pl.BlockSpec((None, DEPTH, A_WIDTH), const3),
            pl.BlockSpec((DEPTH, A_WIDTH), lambda b, i, j: (0, 0)),
            pl.BlockSpec((None, 1, A_WIDTH), const3),
            pl.BlockSpec((None, 1, B_WIDTH), const3),
            pl.BlockSpec((None, len(POOL_WINDOWS), POOL_GW, POOL_GW),
                         lambda b, i, j: (layer, 0, 0, 0)),
            pl.BlockSpec((TM, D_MODEL), h_map),
            pl.BlockSpec((None, D_MODEL, COL_BLK), w_map),
        ],
        out_specs=(pl.BlockSpec((TM, COL_BLK), g_map),
                   pl.BlockSpec((TM, A_WIDTH), y_map),
                   pl.BlockSpec((TM, B_WIDTH), y_map)),
        scratch_shapes=[
            pltpu.VMEM((4, TM, COL_BLK), BF16),
            pltpu.VMEM((2, TM + POOL_HALO, COL_BLK), F32),
            pltpu.VMEM((N_HEADS, HEAD_DIM, HEAD_DIM), F32),
            pltpu.VMEM((N_HEADS, HEAD_DIM, HEAD_DIM), F32),
            pltpu.VMEM((1, A_WIDTH), F32),
            pltpu.VMEM((1, A_WIDTH), F32),
            pltpu.VMEM((N_HEADS, CHUNK, CHUNK), F32),
            pltpu.VMEM((CHUNK, A_WIDTH), F32),
            pltpu.VMEM((CHUNK, A_WIDTH), F32),
            pltpu.VMEM((CHUNK, A_WIDTH), F32),
        ],
        compiler_params=pltpu.CompilerParams(
            dimension_semantics=("arbitrary", "arbitrary", "arbitrary"),
            vmem_limit_bytes=VMEM_LIMIT),
        name="proj_mix",
    )(coef, lower_bounds, hgrn_norm.reshape(DEPTH, 1, A_WIDTH),
      pool_scale.reshape(DEPTH, 1, B_WIDTH), pool_w_b, h, w_in_b)


def _out_kernel(emit_h, ya_ref, yb_ref, ga_ref, gb_ref, x_ref, gate_ref, gain_ref,
                wpa_ref, wpb_ref, wo_ref, ngain_ref, nshift_ref, nscale_ref, o_ref, h_ref=None):
    pa = _dot(ya_ref[...], wpa_ref[...])
    pb = _dot(yb_ref[...], wpb_ref[...])
    merged = (_sigmoid(ga_ref[...].astype(F32)) * pa
              + _sigmoid(gb_ref[...].astype(F32)) * pb)
    out = _dot(merged.astype(BF16), wo_ref[...])
    inv = lax.rsqrt(jnp.mean(out * out, axis=-1, keepdims=True) + EPS)
    xn = x_ref[...] + gate_ref[...] * ((out * inv) * gain_ref[...])
    o_ref[...] = xn
    if emit_h:
        h_ref[...] = _prenorm(xn, ngain_ref[...], nshift_ref[...], nscale_ref[...]).astype(BF16)


def _out_call(layer, ya, yb, g, x2, mod5, norm_post, norm_pre, wpa_b, wpb_b, wo_b, seq):
    t = x2.shape[0]
    tpb = seq // OUT_TM
    emit_h = layer + 1 < DEPTH
    nxt = min(layer + 1, DEPTH - 1)
    resident = dict(pipeline_mode=pl.Buffered(1))
    row_spec = lambda width, cb=0: pl.BlockSpec((OUT_TM, width), lambda i: (i, cb))
    out_shape = [jax.ShapeDtypeStruct((t, D_MODEL), F32)]
    out_specs = [row_spec(D_MODEL)]
    if emit_h:
        out_shape.append(jax.ShapeDtypeStruct((t, D_MODEL), BF16))
        out_specs.append(row_spec(D_MODEL))
    res = pl.pallas_call(
        partial(_out_kernel, emit_h),
        out_shape=tuple(out_shape),
        grid=(t // OUT_TM,),
        in_specs=[
            row_spec(A_WIDTH), row_spec(B_WIDTH),
            row_spec(D_MODEL, 0), row_spec(D_MODEL, 1),
            row_spec(D_MODEL),
            _mod_spec(layer, 2, tpb),
            pl.BlockSpec((None, 1, D_MODEL), lambda i: (layer, 0, 0)),
            pl.BlockSpec((None, A_WIDTH, D_MODEL), lambda i: (layer, 0, 0), **resident),
            pl.BlockSpec((None, B_WIDTH, D_MODEL), lambda i: (layer, 0, 0), **resident),
            pl.BlockSpec((None, D_MODEL, D_MODEL), lambda i: (layer, 0, 0), **resident),
            pl.BlockSpec((None, 1, D_MODEL), lambda i: (nxt, 0, 0)),
            _mod_spec(nxt, 0, tpb), _mod_spec(nxt, 1, tpb),
        ],
        out_specs=tuple(out_specs),
        compiler_params=pltpu.CompilerParams(
            dimension_semantics=("arbitrary",),
            vmem_limit_bytes=VMEM_LIMIT),
        name="out_stage",
    )(ya, yb, g, g, x2, mod5, norm_post.reshape(DEPTH, 1, D_MODEL), wpa_b, wpb_b, wo_b,
      norm_pre.reshape(DEPTH, 1, D_MODEL), mod5, mod5)
    return res if emit_h else (res[0], None)


def kernel(x, c, w_ada, b_ada, norm_pre, norm_post, w_in, lower_bounds, hgrn_norm,
           pool_w, pool_scale, w_proj_a, w_proj_b, w_out):
    bsz, seq, d = x.shape
    assert d == D_MODEL and seq % TM == 0 and seq % OUT_TM == 0 and seq % NORM_TM == 0
    w_in_b = w_in.astype(BF16)
    wpa_b = w_proj_a.astype(BF16)
    wpb_b = w_proj_b.astype(BF16)
    wo_b = w_out.astype(BF16)
    pool_w_b = pool_w.astype(BF16)

    mod = _ada_call(c, w_ada, b_ada)
    mod5 = mod.reshape(DEPTH, bsz, 3, 1, D_MODEL)
    jj = jnp.arange(DEPTH)
    coef = ((jj[None, :] >= 1) & (jj[None, :] <= jj[:, None])).astype(F32)
    coef = jnp.broadcast_to(coef[:, :, None], (DEPTH, DEPTH, A_WIDTH))

    x2 = x.reshape(bsz * seq, D_MODEL)
    h = _norm_call(x2, mod5, norm_pre, seq)
    for layer in range(DEPTH):
        g, ya, yb = _proj_mix_call(layer, h, coef, lower_bounds, hgrn_norm, pool_scale,
                                   pool_w_b, w_in_b, bsz, seq)
        x2, h = _out_call(layer, ya, yb, g, x2, mod5, norm_post, norm_pre,
                          wpa_b, wpb_b, wo_b, seq)
    return x2.reshape(bsz, seq, D_MODEL)
```

```python
from functools import partial

import jax
import jax.numpy as jnp
from jax import lax
from jax.experimental import pallas as pl
from jax.experimental.pallas import tpu as pltpu

D_MODEL = 2048
DEPTH = 4
CHUNK = 64
EPS = 1e-6
MIN_FORGET = 1e-30
HEAD_DIM = 128
A_WIDTH = D_MODEL // 2
N_HEADS = A_WIDTH // HEAD_DIM
B_WIDTH = D_MODEL // 2
POOL_WINDOWS = (2, 4, 8, 16)
POOL_GW = B_WIDTH // len(POOL_WINDOWS)
POOL_HALO = 16
IN_COLS = 6 * A_WIDTH + 2 * D_MODEL
COL_BLK = 1024
N_COL_BLK = IN_COLS // COL_BLK
N_GATE_BLK = 2 * D_MODEL // COL_BLK
FIRST_GATE_BLK = 6 * A_WIDTH // COL_BLK
SLOT_Q, SLOT_VA, SLOT_ZA, SLOT_ZB = 0, 1, 2, 3
SLOT_F, SLOT_VB = 0, 1

SAFE_CHUNK_LOG_DECAY = 120.0

TM = 1024
CHUNKS_PER_STEP = TM // CHUNK // N_GATE_BLK
POOL_ROWS = TM // N_GATE_BLK
OUT_TM = 512
NORM_TM = 1024
VMEM_LIMIT = 60 * 1024 * 1024

F32 = jnp.float32
BF16 = jnp.bfloat16


LOG2_E = 1.4426950408889634


def _sigmoid(v):
    return 0.5 * jnp.tanh(0.5 * v) + 0.5


def _silu(v):
    hv = 0.5 * v
    return hv * jnp.tanh(hv) + hv


def _dot(a, b):
    return jnp.dot(a, b, preferred_element_type=F32)


def _dot_nt(a, b):
    return lax.dot_general(a, b, (((1,), (1,)), ((), ())), preferred_element_type=F32)


def _dot_tn(a, b):
    return lax.dot_general(a, b, (((0,), (0,)), ((), ())), preferred_element_type=F32)


def _prenorm(x, gain, shift, scale):
    inv = lax.rsqrt(jnp.mean(x * x, axis=-1, keepdims=True) + EPS)
    return ((x * inv) * gain) * (1.0 + scale) + shift


def _ada_kernel(ct_ref, w_ref, b_ref, o_ref):
    ct = ct_ref[...]
    ca = _silu(ct)
    w = w_ref[...]
    rows = [jnp.sum(w * ca[:, b:b + 1], axis=0, keepdims=True)
            for b in range(ct.shape[1])]
    o_ref[...] = jnp.concatenate(rows, axis=0) + b_ref[...]


def _ada_call(c, w_ada, b_ada):
    bsz = c.shape[0]
    tn = 1024
    n3 = 3 * D_MODEL
    return pl.pallas_call(
        _ada_kernel,
        out_shape=jax.ShapeDtypeStruct((DEPTH, bsz, n3), F32),
        grid=(DEPTH, n3 // tn),
        in_specs=[
            pl.BlockSpec((D_MODEL, bsz), lambda l, j: (0, 0)),
            pl.BlockSpec((None, D_MODEL, tn), lambda l, j: (l, 0, j)),
            pl.BlockSpec((None, 1, tn), lambda l, j: (l, 0, j)),
        ],
        out_specs=pl.BlockSpec((None, bsz, tn), lambda l, j: (l, 0, j)),
        compiler_params=pltpu.CompilerParams(
            dimension_semantics=("arbitrary", "arbitrary"),
            vmem_limit_bytes=VMEM_LIMIT),
        name="ada_mod",
    )(c.T, w_ada, b_ada.reshape(DEPTH, 1, n3))


def _norm_kernel(x_ref, gain_ref, shift_ref, scale_ref, h_ref):
    h_ref[...] = _prenorm(x_ref[...], gain_ref[...], shift_ref[...], scale_ref[...]).astype(BF16)


def _mod_spec(layer, which, tiles_per_batch):
    return pl.BlockSpec((None, None, None, 1, D_MODEL),
                        lambda i: (layer, i // tiles_per_batch, which, 0, 0))


def _norm_call(x2, mod5, norm_pre, seq):
    t = x2.shape[0]
    tpb = seq // NORM_TM
    return pl.pallas_call(
        _norm_kernel,
        out_shape=jax.ShapeDtypeStruct((t, D_MODEL), BF16),
        grid=(t // NORM_TM,),
        in_specs=[
            pl.BlockSpec((NORM_TM, D_MODEL), lambda i: (i, 0)),
            pl.BlockSpec((None, 1, D_MODEL), lambda i: (0, 0, 0)),
            _mod_spec(0, 0, tpb), _mod_spec(0, 1, tpb),
        ],
        out_specs=pl.BlockSpec((NORM_TM, D_MODEL), lambda i: (i, 0)),
        compiler_params=pltpu.CompilerParams(
            dimension_semantics=("arbitrary",), vmem_limit_bytes=VMEM_LIMIT),
        name="pre_norm",
    )(x2, norm_pre.reshape(DEPTH, 1, D_MODEL), mod5, mod5)


def _split3(v):
    hi = v.astype(BF16)
    r1 = v - hi.astype(F32)
    mid = r1.astype(BF16)
    lo = (r1 - mid.astype(F32)).astype(BF16)
    return hi, mid, lo


def _hgrn_chunk_stages(row0, lb, one_m_lb, hn, tri, causal, col, pb_ref, pf_ref, ya_ref,
                       st_ref, b_last_out, direct_refs=None):
    rows = pl.ds(row0, CHUNK)
    a = pf_ref[SLOT_F, pl.ds(row0 + POOL_HALO, CHUNK), :]
    half_gap = 0.5 * one_m_lb
    gt = half_gap * jnp.tanh(0.5 * a)
    f = (lb + half_gap) + gt
    kk = half_gap - gt
    lf = jnp.log(jnp.maximum(f, MIN_FORGET))
    q = _silu(pb_ref[SLOT_Q, rows, :].astype(F32))
    yield

    hi, mid, lo = _split3(lf)
    yield

    b = _dot(tri, jnp.concatenate([hi, mid, lo], axis=0))
    yield

    b_last = b[CHUNK - 1:CHUNK, :]
    b_last_out.append(b_last)
    st_decay = jnp.exp(b_last)
    if direct_refs is None:
        half = 0.5 * b_last
        d2 = (b - half) * LOG2_E
        q_in = q * jnp.exp2(d2)
        k_in = kk * jnp.exp2(-d2)
        e_half = jnp.exp(half)
        q_st = (q_in * e_half).astype(BF16)
        k_st = (k_in * e_half).astype(BF16)
        q_in = q_in.astype(BF16)
        k_in = k_in.astype(BF16)
    else:
        q_st = (q * jnp.exp(b)).astype(BF16)
        k_st = (kk * jnp.exp(b_last - b)).astype(BF16)
    yield

    heads = [slice(h * HEAD_DIM, (h + 1) * HEAD_DIM) for h in range(N_HEADS)]
    if direct_refs is None:
        scores = [_dot_nt(q_in[:, cs], k_in[:, cs]) for cs in heads]
    else:
        a_ref, bs_ref, qs_ref, ks_ref = direct_refs
        bs_ref[...] = b
        qs_ref[...] = q
        ks_ref[...] = kk
        a_ref[...] = jnp.zeros_like(a_ref)

        def src_body(s, c2):
            brow = bs_ref[pl.ds(s, 1), :]
            krow = ks_ref[pl.ds(s, 1), :]
            p = jnp.exp(jnp.minimum(bs_ref[...] - brow, 0.0)) * qs_ref[...] * krow
            for h, cs in enumerate(heads):
                colv = jnp.sum(p[:, cs], axis=1, keepdims=True)
                a_ref[h] = jnp.where(col == s, colv, a_ref[h])
            return c2

        lax.fori_loop(0, CHUNK, src_body, 0)
        scores = [a_ref[h] for h in range(N_HEADS)]
    yield

    scores = [jnp.where(causal, sc, 0.0).astype(BF16) for sc in scores]
    yield

    outs, updates = [], []
    for h, cs in enumerate(heads):
        v = pb_ref[SLOT_VA, rows, cs]
        outs.append(_dot_nt(q_st[:, cs], st_ref[h].astype(BF16)) + _dot(scores[h], v))
        updates.append(_dot_tn(v, k_st[:, cs]))
    yield

    for h, cs in enumerate(heads):
        if h == N_HEADS // 2:
            yield
        st_ref[h] = st_ref[h] * st_decay[:, cs] + updates[h]
        o = outs[h]
        o = o * lax.rsqrt(jnp.mean(o * o, axis=-1, keepdims=True) + EPS)
        z = pb_ref[SLOT_ZA, rows, cs].astype(F32)
        ya_ref[rows, cs] = (o * hn[:, cs] * _silu(z)).astype(BF16)


def _pool_group_stages(g, row0, seq_row0, pb_ref, pf_ref, pw_ref, pscale_ref, yb_ref):
    w = POOL_WINDOWS[g]
    n = POOL_ROWS + POOL_HALO
    pos = (seq_row0 + row0 + lax.broadcasted_iota(jnp.int32, (POOL_ROWS, POOL_GW), 0) + 1
           ).astype(F32)
    rows = pl.ds(row0, POOL_ROWS)
    cs = slice(g * POOL_GW, (g + 1) * POOL_GW)
    ext = pf_ref[SLOT_VB, pl.ds(row0, n), cs]
    tot = ext
    d = 1
    while d < w:
        tot = tot + pltpu.roll(tot, d, axis=0)
        d *= 2
    cur = ext[POOL_HALO:, :]
    pooled = (tot[POOL_HALO:, :] / jnp.minimum(pos, float(w)) - cur).astype(BF16)
    yield

    mixed = _dot(pooled, pw_ref[g])
    yield

    z = pb_ref[SLOT_ZB, rows, cs].astype(F32)
    yb_ref[rows, cs] = (mixed * pscale_ref[:, cs] * _silu(z)).astype(BF16)


FUSED_SLOTS = (
    (0, 0),
    (0, 1, 1),
    (1, 0, 'p0'),
    (0, 'p0', 1, 2, 2),
    (1, 2, 0, 'p0', 'p1'),
    (0, 'p1', 1, 2, 3),
    (2, 0, 0, 3, 'p1'),
    (1, 3, 2, 'p2'),
    ('p2', 1, 1, 3),
    (2, 3, 'p2', 'p3'),
    ('p3', 2, 2, 3),
    (3, 'p3'),
    (3, 3),
)
MM_ROW_SPLIT = 4
MM_COL_SPLIT = 4


def _proj_mix_kernel(tiles_per_row, coef_ref, lbraw_ref, hn_ref, pscale_ref, pw_ref, h_ref, w_ref,
                     g_ref, ya_ref, yb_ref,
                     pb_ref, pf_ref, st_ref, snap_ref, minb_ref, lb_ref,
                     a_ref, bs_ref, qs_ref, ks_ref):
    i = pl.program_id(0)
    j = pl.program_id(1)
    nt = pl.num_programs(0) - 1
    row_tile = (i - 1) % tiles_per_row
    has_mm = i < nt
    has_mix = i > 0
    is_gate = j < N_GATE_BLK

    def lower_bound():
        lbr = lbraw_ref[...]
        e = jnp.exp(lbr - jnp.max(lbr, axis=0, keepdims=True))
        sm = e / jnp.sum(e, axis=0, keepdims=True)
        return jnp.sum(coef_ref[...] * sm, axis=0, keepdims=True)

    def masks():
        row = lax.broadcasted_iota(jnp.int32, (CHUNK, CHUNK), 0)
        col = lax.broadcasted_iota(jnp.int32, (CHUNK, CHUNK), 1)
        causal = row >= col
        tri = causal.astype(BF16)
        return jnp.concatenate([tri, tri, tri], axis=1), causal, col

    def mm_slices():
        mb, nb = TM // MM_ROW_SPLIT, COL_BLK // MM_COL_SPLIT
        for c in range(MM_COL_SPLIT):
            for r in range(MM_ROW_SPLIT):
                rs, cs = slice(r * mb, (r + 1) * mb), slice(c * nb, (c + 1) * nb)
                g_ref[rs, cs] = _dot(h_ref[rs, :], w_ref[:, cs]).astype(BF16)
                yield

    def mix_step(with_mm):
        lb = lb_ref[...]
        tri, causal, col = masks()
        hn = hn_ref[...]
        b_lasts = []
        chunks = [
            _hgrn_chunk_stages(pl.multiple_of((j * CHUNKS_PER_STEP + cc) * CHUNK, CHUNK),
                               lb, 1.0 - lb, hn, tri, causal, col, pb_ref, pf_ref, ya_ref,
                               st_ref, b_lasts)
            for cc in range(CHUNKS_PER_STEP)]
        pools = {
            'p%d' % g: _pool_group_stages(g, pl.multiple_of(j * POOL_ROWS, POOL_ROWS),
                                          row_tile * TM, pb_ref, pf_ref, pw_ref, pscale_ref,
                                          yb_ref)
            for g in range(len(POOL_WINDOWS))}
        slices = mm_slices() if with_mm else iter(())
        for slot in FUSED_SLOTS:
            for item in slot:
                next(pools[item] if isinstance(item, str) else chunks[item], None)
            next(slices, None)
        for _ in slices:
            pass
        done = object()
        assert all(next(gen, done) is done for gen in [*chunks, *pools.values()])
        minb = minb_ref[...]
        for b_last in b_lasts:
            minb = jnp.minimum(minb, b_last)
        minb_ref[...] = minb

    @pl.when(has_mix & (j == 0))
    def _():
        @pl.when(row_tile == 0)
        def _():
            st_ref[...] = jnp.zeros_like(st_ref)
            pf_ref[SLOT_VB, 0:POOL_HALO, :] = jnp.zeros((POOL_HALO, COL_BLK), F32)

        snap_ref[...] = st_ref[...]
        minb_ref[...] = jnp.zeros_like(minb_ref)
        lb_ref[...] = lower_bound()

    @pl.when(is_gate & has_mm & has_mix)
    def _():
        mix_step(with_mm=True)

    @pl.when(is_gate & has_mm & jnp.logical_not(has_mix))
    def _():
        g_ref[...] = _dot(h_ref[...], w_ref[...]).astype(BF16)

    @pl.when(is_gate & jnp.logical_not(has_mm))
    def _():
        mix_step(with_mm=False)

    is_f32_blk = (j == N_GATE_BLK + 1) | (j == N_GATE_BLK + 4)

    @pl.when(jnp.logical_not(is_gate) & has_mm & jnp.logical_not(is_f32_blk))
    def _():
        jj = j - N_GATE_BLK
        slot = jnp.where(jj == 0, SLOT_Q, jnp.where(jj == 5, SLOT_ZB, jj - 1))
        pb_ref[slot] = _dot(h_ref[...], w_ref[...]).astype(BF16)

    @pl.when(jnp.logical_not(is_gate) & has_mm & is_f32_blk)
    def _():
        slot = jnp.where(j == N_GATE_BLK + 1, SLOT_F, SLOT_VB)
        pf_ref[slot, POOL_HALO:POOL_HALO + TM, :] = _dot(h_ref[...], w_ref[...])

    @pl.when(has_mix & (j == N_GATE_BLK - 1))
    def _():
        pf_ref[SLOT_VB, 0:POOL_HALO, :] = pf_ref[SLOT_VB, TM:TM + POOL_HALO, :]

        @pl.when(jnp.min(minb_ref[...]) < -SAFE_CHUNK_LOG_DECAY)
        def _():
            st_ref[...] = snap_ref[...]
            lb = lb_ref[...]
            tri, causal, col = masks()
            hn = hn_ref[...]

            def redo(ci, carry):
                for _ in _hgrn_chunk_stages(
                        pl.multiple_of(ci * CHUNK, CHUNK), lb, 1.0 - lb, hn, tri, causal, col,
                        pb_ref, pf_ref, ya_ref, st_ref, [],
                        direct_refs=(a_ref, bs_ref, qs_ref, ks_ref)):
                    pass
                return carry

            lax.fori_loop(0, TM // CHUNK, redo, 0)


def _proj_mix_call(layer, h, coef, lower_bounds, hgrn_norm, pool_scale, pool_w_b, w_in_b,
                   bsz, seq):
    t = h.shape[0]
    nt = t // TM

    def h_map(i, j):
        return (jnp.minimum(i, nt - 1), 0)

    def w_map(i, j):
        cb = (j + FIRST_GATE_BLK) % N_COL_BLK
        return (layer, 0, jnp.where(i == nt, FIRST_GATE_BLK - 1, cb))

    def g_map(i, j):
        gb = jnp.minimum(j, N_GATE_BLK - 1)
        return (jnp.minimum(i, nt - 1), jnp.where(i == nt, N_GATE_BLK - 1, gb))

    def y_map(i, j):
        return (jnp.maximum(i - 1, 0), 0)

    const3 = lambda i, j: (layer, 0, 0)
    return pl.pallas_call(
        partial(_proj_mix_kernel, seq // TM),
        out_shape=(jax.ShapeDtypeStruct((t, 2 * D_MODEL), BF16),
                   jax.ShapeDtypeStruct((t, A_WIDTH), BF16),
                   jax.ShapeDtypeStruct((t, B_WIDTH), BF16)),
        grid=(nt + 1, N_COL_BLK),
        in_specs=[
            pl.BlockSpec((None, DEPTH, A_WIDTH), const3),
            pl.BlockSpec((DEPTH, A_WIDTH), lambda i, j: (0, 0)),
            pl.BlockSpec((None, 1, A_WIDTH), const3),
            pl.BlockSpec((None, 1, B_WIDTH), const3),
            pl.BlockSpec((None, len(POOL_WINDOWS), POOL_GW, POOL_GW),
                         lambda i, j: (layer, 0, 0, 0)),
            pl.BlockSpec((TM, D_MODEL), h_map),
            pl.BlockSpec((None, D_MODEL, COL_BLK), w_map),
        ],
        out_specs=(pl.BlockSpec((TM, COL_BLK), g_map),
                   pl.BlockSpec((TM, A_WIDTH), y_map),
                   pl.BlockSpec((TM, B_WIDTH), y_map)),
        scratch_shapes=[
            pltpu.VMEM((4, TM, COL_BLK), BF16),
            pltpu.VMEM((2, TM + POOL_HALO, COL_BLK), F32),
            pltpu.VMEM((N_HEADS, HEAD_DIM, HEAD_DIM), F32),
            pltpu.VMEM((N_HEADS, HEAD_DIM, HEAD_DIM), F32),
            pltpu.VMEM((1, A_WIDTH), F32),
            pltpu.VMEM((1, A_WIDTH), F32),
            pltpu.VMEM((N_HEADS, CHUNK, CHUNK), F32),
            pltpu.VMEM((CHUNK, A_WIDTH), F32),
            pltpu.VMEM((CHUNK, A_WIDTH), F32),
            pltpu.VMEM((CHUNK, A_WIDTH), F32),
        ],
        compiler_params=pltpu.CompilerParams(
            dimension_semantics=("arbitrary", "arbitrary"),
            vmem_limit_bytes=VMEM_LIMIT),
        name="proj_mix",
    )(coef, lower_bounds, hgrn_norm.reshape(DEPTH, 1, A_WIDTH),
      pool_scale.reshape(DEPTH, 1, B_WIDTH), pool_w_b, h, w_in_b)


def _out_kernel(emit_h, ya_ref, yb_ref, ga_ref, gb_ref, x_ref, gate_ref, gain_ref,
                wpa_ref, wpb_ref, wo_ref, ngain_ref, nshift_ref, nscale_ref, o_ref, h_ref=None):
    pa = _dot(ya_ref[...], wpa_ref[...])
    pb = _dot(yb_ref[...], wpb_ref[...])
    merged = (_sigmoid(ga_ref[...].astype(F32)) * pa
              + _sigmoid(gb_ref[...].astype(F32)) * pb)
    out = _dot(merged.astype(BF16), wo_ref[...])
    inv = lax.rsqrt(jnp.mean(out * out, axis=-1, keepdims=True) + EPS)
    xn = x_ref[...] + (out * inv) * (gate_ref[...] * gain_ref[...])
    o_ref[...] = xn
    if emit_h:
        inv2 = lax.rsqrt(jnp.mean(xn * xn, axis=-1, keepdims=True) + EPS)
        gain2 = ngain_ref[...] * (1.0 + nscale_ref[...])
        h_ref[...] = ((xn * inv2) * gain2 + nshift_ref[...]).astype(BF16)


def _out_call(layer, ya, yb, g, x2, mod5, norm_post, norm_pre, wpa_b, wpb_b, wo_b, seq):
    t = x2.shape[0]
    tpb = seq // OUT_TM
    emit_h = layer + 1 < DEPTH
    nxt = min(layer + 1, DEPTH - 1)
    resident = dict(pipeline_mode=pl.Buffered(1))
    row_spec = lambda width, cb=0: pl.BlockSpec((OUT_TM, width), lambda i: (i, cb))
    out_shape = [jax.ShapeDtypeStruct((t, D_MODEL), F32)]
    out_specs = [row_spec(D_MODEL)]
    if emit_h:
        out_shape.append(jax.ShapeDtypeStruct((t, D_MODEL), BF16))
        out_specs.append(row_spec(D_MODEL))
    res = pl.pallas_call(
        partial(_out_kernel, emit_h),
        out_shape=tuple(out_shape),
        grid=(t // OUT_TM,),
        in_specs=[
            row_spec(A_WIDTH), row_spec(B_WIDTH),
            row_spec(D_MODEL, 0), row_spec(D_MODEL, 1),
            row_spec(D_MODEL),
            _mod_spec(layer, 2, tpb),
            pl.BlockSpec((None, 1, D_MODEL), lambda i: (layer, 0, 0)),
            pl.BlockSpec((None, A_WIDTH, D_MODEL), lambda i: (layer, 0, 0), **resident),
            pl.BlockSpec((None, B_WIDTH, D_MODEL), lambda i: (layer, 0, 0), **resident),
            pl.BlockSpec((None, D_MODEL, D_MODEL), lambda i: (layer, 0, 0), **resident),
            pl.BlockSpec((None, 1, D_MODEL), lambda i: (nxt, 0, 0)),
            _mod_spec(nxt, 0, tpb), _mod_spec(nxt, 1, tpb),
        ],
        out_specs=tuple(out_specs),
        compiler_params=pltpu.CompilerParams(
            dimension_semantics=("arbitrary",),
            vmem_limit_bytes=VMEM_LIMIT),
        name="out_stage",
    )(ya, yb, g, g, x2, mod5, norm_post.reshape(DEPTH, 1, D_MODEL), wpa_b, wpb_b, wo_b,
      norm_pre.reshape(DEPTH, 1, D_MODEL), mod5, mod5)
    return res if emit_h else (res[0], None)


def kernel(x, c, w_ada, b_ada, norm_pre, norm_post, w_in, lower_bounds, hgrn_norm,
           pool_w, pool_scale, w_proj_a, w_proj_b, w_out):
    bsz, seq, d = x.shape
    assert d == D_MODEL and seq % TM == 0 and seq % OUT_TM == 0 and seq % NORM_TM == 0
    w_in_b = w_in.astype(BF16)
    wpa_b = w_proj_a.astype(BF16)
    wpb_b = w_proj_b.astype(BF16)
    wo_b = w_out.astype(BF16)
    pool_w_b = pool_w.astype(BF16)

    mod = _ada_call(c, w_ada, b_ada)
    mod5 = mod.reshape(DEPTH, bsz, 3, 1, D_MODEL)
    jj = jnp.arange(DEPTH)
    coef = ((jj[None, :] >= 1) & (jj[None, :] <= jj[:, None])).astype(F32)
    coef = jnp.broadcast_to(coef[:, :, None], (DEPTH, DEPTH, A_WIDTH))

    x2 = x.reshape(bsz * seq, D_MODEL)
    h = _norm_call(x2, mod5, norm_pre, seq)
    for layer in range(DEPTH):
        g, ya, yb = _proj_mix_call(layer, h, coef, lower_bounds, hgrn_norm, pool_scale,
                                   pool_w_b, w_in_b, bsz, seq)
        x2, h = _out_call(layer, ya, yb, g, x2, mod5, norm_post, norm_pre,
                          wpa_b, wpb_b, wo_b, seq)
    return x2.reshape(bsz, seq, D_MODEL)
```

```python
from functools import partial

import jax
import jax.numpy as jnp
from jax import lax
from jax.experimental import pallas as pl
from jax.experimental.pallas import tpu as pltpu

D_MODEL = 2048
DEPTH = 4
CHUNK = 64
EPS = 1e-6
MIN_FORGET = 1e-30
HEAD_DIM = 128
A_WIDTH = D_MODEL // 2
N_HEADS = A_WIDTH // HEAD_DIM
B_WIDTH = D_MODEL // 2
POOL_WINDOWS = (2, 4, 8, 16)
POOL_GW = B_WIDTH // len(POOL_WINDOWS)
POOL_HALO = 16
IN_COLS = 6 * A_WIDTH + 2 * D_MODEL
COL_BLK = 1024
N_COL_BLK = IN_COLS // COL_BLK
N_GATE_BLK = 2 * D_MODEL // COL_BLK
FIRST_GATE_BLK = 6 * A_WIDTH // COL_BLK
SLOT_Q, SLOT_VA, SLOT_ZA, SLOT_ZB = 0, 1, 2, 3
SLOT_F, SLOT_VB = 0, 1

SAFE_CHUNK_LOG_DECAY = 120.0

TM = 1024
CHUNKS_PER_STEP = TM // CHUNK // N_GATE_BLK
POOL_ROWS = TM // N_GATE_BLK
OUT_TM = 512
OUT_COL_SLICE = 256
NORM_TM = 1024
VMEM_LIMIT = 60 * 1024 * 1024

F32 = jnp.float32
BF16 = jnp.bfloat16


LOG2_E = 1.4426950408889634


def _sigmoid(v):
    return 0.5 * jnp.tanh(0.5 * v) + 0.5


def _silu(v):
    hv = 0.5 * v
    return hv * jnp.tanh(hv) + hv


def _dot(a, b):
    return jnp.dot(a, b, preferred_element_type=F32)


def _dot_nt(a, b):
    return lax.dot_general(a, b, (((1,), (1,)), ((), ())), preferred_element_type=F32)


def _dot_tn(a, b):
    return lax.dot_general(a, b, (((0,), (0,)), ((), ())), preferred_element_type=F32)


def _prenorm(x, gain, shift, scale):
    inv = lax.rsqrt(jnp.mean(x * x, axis=-1, keepdims=True) + EPS)
    return ((x * inv) * gain) * (1.0 + scale) + shift


def _ada_kernel(ct_ref, w_ref, b_ref, o_ref):
    ct = ct_ref[...]
    ca = _silu(ct)
    w = w_ref[...]
    rows = [jnp.sum(w * ca[:, b:b + 1], axis=0, keepdims=True)
            for b in range(ct.shape[1])]
    o_ref[...] = jnp.concatenate(rows, axis=0) + b_ref[...]


def _ada_call(c, w_ada, b_ada):
    bsz = c.shape[0]
    tn = 1024
    n3 = 3 * D_MODEL
    return pl.pallas_call(
        _ada_kernel,
        out_shape=jax.ShapeDtypeStruct((DEPTH, bsz, n3), F32),
        grid=(DEPTH, n3 // tn),
        in_specs=[
            pl.BlockSpec((D_MODEL, bsz), lambda l, j: (0, 0)),
            pl.BlockSpec((None, D_MODEL, tn), lambda l, j: (l, 0, j)),
            pl.BlockSpec((None, 1, tn), lambda l, j: (l, 0, j)),
        ],
        out_specs=pl.BlockSpec((None, bsz, tn), lambda l, j: (l, 0, j)),
        compiler_params=pltpu.CompilerParams(
            dimension_semantics=("arbitrary", "arbitrary"),
            vmem_limit_bytes=VMEM_LIMIT),
        name="ada_mod",
    )(c.T, w_ada, b_ada.reshape(DEPTH, 1, n3))


def _norm_kernel(x_ref, gain_ref, shift_ref, scale_ref, h_ref):
    h_ref[...] = _prenorm(x_ref[...], gain_ref[...], shift_ref[...], scale_ref[...]).astype(BF16)


def _mod_spec(layer, which, tiles_per_batch):
    return pl.BlockSpec((None, None, None, 1, D_MODEL),
                        lambda i: (layer, i // tiles_per_batch, which, 0, 0))


def _norm_call(x2, mod5, norm_pre, seq):
    t = x2.shape[0]
    tpb = seq // NORM_TM
    return pl.pallas_call(
        _norm_kernel,
        out_shape=jax.ShapeDtypeStruct((t, D_MODEL), BF16),
        grid=(t // NORM_TM,),
        in_specs=[
            pl.BlockSpec((NORM_TM, D_MODEL), lambda i: (i, 0)),
            pl.BlockSpec((None, 1, D_MODEL), lambda i: (0, 0, 0)),
            _mod_spec(0, 0, tpb), _mod_spec(0, 1, tpb),
        ],
        out_specs=pl.BlockSpec((NORM_TM, D_MODEL), lambda i: (i, 0)),
        compiler_params=pltpu.CompilerParams(
            dimension_semantics=("arbitrary",), vmem_limit_bytes=VMEM_LIMIT),
        name="pre_norm",
    )(x2, norm_pre.reshape(DEPTH, 1, D_MODEL), mod5, mod5)


def _split3(v):
    hi = v.astype(BF16)
    r1 = v - hi.astype(F32)
    mid = r1.astype(BF16)
    lo = (r1 - mid.astype(F32)).astype(BF16)
    return hi, mid, lo


def _hgrn_chunk_stages(row0, lb, one_m_lb, hn, tri, causal, col, pb_ref, pf_ref, ya_ref,
                       st_ref, b_last_out, direct_refs=None):
    rows = pl.ds(row0, CHUNK)
    a = pf_ref[SLOT_F, pl.ds(row0 + POOL_HALO, CHUNK), :]
    half_gap = 0.5 * one_m_lb
    gt = half_gap * jnp.tanh(0.5 * a)
    f = (lb + half_gap) + gt
    kk = half_gap - gt
    lf = jnp.log(jnp.maximum(f, MIN_FORGET))
    q = _silu(pb_ref[SLOT_Q, rows, :].astype(F32))
    yield

    hi, mid, lo = _split3(lf)
    yield

    b = _dot(tri, jnp.concatenate([hi, mid, lo], axis=0))
    yield

    b_last = b[CHUNK - 1:CHUNK, :]
    b_last_out.append(b_last)
    st_decay = jnp.exp(b_last)
    if direct_refs is None:
        half = 0.5 * b_last
        d2 = (b - half) * LOG2_E
        q_in = q * jnp.exp2(d2)
        k_in = kk * jnp.exp2(-d2)
        e_half = jnp.exp(half)
        q_st = (q_in * e_half).astype(BF16)
        k_st = (k_in * e_half).astype(BF16)
        q_in = q_in.astype(BF16)
        k_in = k_in.astype(BF16)
    else:
        q_st = (q * jnp.exp(b)).astype(BF16)
        k_st = (kk * jnp.exp(b_last - b)).astype(BF16)
    yield

    heads = [slice(h * HEAD_DIM, (h + 1) * HEAD_DIM) for h in range(N_HEADS)]
    if direct_refs is None:
        scores = [_dot_nt(q_in[:, cs], k_in[:, cs]) for cs in heads]
    else:
        a_ref, bs_ref, qs_ref, ks_ref = direct_refs
        bs_ref[...] = b
        qs_ref[...] = q
        ks_ref[...] = kk
        a_ref[...] = jnp.zeros_like(a_ref)

        def src_body(s, c2):
            brow = bs_ref[pl.ds(s, 1), :]
            krow = ks_ref[pl.ds(s, 1), :]
            p = jnp.exp(jnp.minimum(bs_ref[...] - brow, 0.0)) * qs_ref[...] * krow
            for h, cs in enumerate(heads):
                colv = jnp.sum(p[:, cs], axis=1, keepdims=True)
                a_ref[h] = jnp.where(col == s, colv, a_ref[h])
            return c2

        lax.fori_loop(0, CHUNK, src_body, 0)
        scores = [a_ref[h] for h in range(N_HEADS)]
    yield

    scores = [jnp.where(causal, sc, 0.0).astype(BF16) for sc in scores]
    yield

    outs, updates = [], []
    for h, cs in enumerate(heads):
        v = pb_ref[SLOT_VA, rows, cs]
        outs.append(_dot_nt(q_st[:, cs], st_ref[h].astype(BF16)) + _dot(scores[h], v))
        updates.append(_dot_tn(v, k_st[:, cs]))
    yield

    for h, cs in enumerate(heads):
        if h == N_HEADS // 2:
            yield
        st_ref[h] = st_ref[h] * st_decay[:, cs] + updates[h]
        o = outs[h]
        o = o * lax.rsqrt(jnp.mean(o * o, axis=-1, keepdims=True) + EPS)
        z = pb_ref[SLOT_ZA, rows, cs].astype(F32)
        ya_ref[rows, cs] = (o * hn[:, cs] * _silu(z)).astype(BF16)


def _pool_group_stages(g, row0, seq_row0, pb_ref, pf_ref, pw_ref, pscale_ref, yb_ref):
    w = POOL_WINDOWS[g]
    n = POOL_ROWS + POOL_HALO
    pos = (seq_row0 + row0 + lax.broadcasted_iota(jnp.int32, (POOL_ROWS, POOL_GW), 0) + 1
           ).astype(F32)
    rows = pl.ds(row0, POOL_ROWS)
    cs = slice(g * POOL_GW, (g + 1) * POOL_GW)
    ext = pf_ref[SLOT_VB, pl.ds(row0, n), cs]
    tot = ext
    d = 1
    while d < w:
        tot = tot + pltpu.roll(tot, d, axis=0)
        d *= 2
    cur = ext[POOL_HALO:, :]
    pooled = (tot[POOL_HALO:, :] / jnp.minimum(pos, float(w)) - cur).astype(BF16)
    yield

    mixed = _dot(pooled, pw_ref[g])
    yield

    z = pb_ref[SLOT_ZB, rows, cs].astype(F32)
    yb_ref[rows, cs] = (mixed * pscale_ref[:, cs] * _silu(z)).astype(BF16)


FUSED_SLOTS = (
    (0, 0),
    (0, 1, 1),
    (1, 0, 'p0'),
    (0, 'p0', 1, 2, 2),
    (1, 2, 0, 'p0', 'p1'),
    (0, 'p1', 1, 2, 3),
    (2, 0, 0, 3, 'p1'),
    (1, 3, 2, 'p2'),
    ('p2', 1, 1, 3),
    (2, 3, 'p2', 'p3'),
    ('p3', 2, 2, 3),
    (3, 'p3'),
    (3, 3),
)
MM_ROW_SPLIT = 4
MM_COL_SPLIT = 4


def _proj_mix_kernel(tiles_per_row, coef_ref, lbraw_ref, hn_ref, pscale_ref, pw_ref, h_ref, w_ref,
                     g_ref, ya_ref, yb_ref,
                     pb_ref, pf_ref, st_ref, snap_ref, minb_ref, lb_ref,
                     a_ref, bs_ref, qs_ref, ks_ref):
    i = pl.program_id(0)
    j = pl.program_id(1)
    nt = pl.num_programs(0) - 1
    row_tile = (i - 1) % tiles_per_row
    has_mm = i < nt
    has_mix = i > 0
    is_gate = j < N_GATE_BLK

    def lower_bound():
        lbr = lbraw_ref[...]
        e = jnp.exp(lbr - jnp.max(lbr, axis=0, keepdims=True))
        sm = e / jnp.sum(e, axis=0, keepdims=True)
        return jnp.sum(coef_ref[...] * sm, axis=0, keepdims=True)

    def masks():
        row = lax.broadcasted_iota(jnp.int32, (CHUNK, CHUNK), 0)
        col = lax.broadcasted_iota(jnp.int32, (CHUNK, CHUNK), 1)
        causal = row >= col
        tri = causal.astype(BF16)
        return jnp.concatenate([tri, tri, tri], axis=1), causal, col

    def mm_slices():
        mb, nb = TM // MM_ROW_SPLIT, COL_BLK // MM_COL_SPLIT
        for c in range(MM_COL_SPLIT):
            for r in range(MM_ROW_SPLIT):
                rs, cs = slice(r * mb, (r + 1) * mb), slice(c * nb, (c + 1) * nb)
                g_ref[rs, cs] = _dot(h_ref[rs, :], w_ref[:, cs]).astype(BF16)
                yield

    def mix_step(with_mm):
        lb = lb_ref[...]
        tri, causal, col = masks()
        hn = hn_ref[...]
        b_lasts = []
        chunks = [
            _hgrn_chunk_stages(pl.multiple_of((j * CHUNKS_PER_STEP + cc) * CHUNK, CHUNK),
                               lb, 1.0 - lb, hn, tri, causal, col, pb_ref, pf_ref, ya_ref,
                               st_ref, b_lasts)
            for cc in range(CHUNKS_PER_STEP)]
        pools = {
            'p%d' % g: _pool_group_stages(g, pl.multiple_of(j * POOL_ROWS, POOL_ROWS),
                                          row_tile * TM, pb_ref, pf_ref, pw_ref, pscale_ref,
                                          yb_ref)
            for g in range(len(POOL_WINDOWS))}
        slices = mm_slices() if with_mm else iter(())
        for slot in FUSED_SLOTS:
            for item in slot:
                next(pools[item] if isinstance(item, str) else chunks[item], None)
            next(slices, None)
        for _ in slices:
            pass
        done = object()
        assert all(next(gen, done) is done for gen in [*chunks, *pools.values()])
        minb = minb_ref[...]
        for b_last in b_lasts:
            minb = jnp.minimum(minb, b_last)
        minb_ref[...] = minb

    @pl.when(has_mix & (j == 0))
    def _():
        @pl.when(row_tile == 0)
        def _():
            st_ref[...] = jnp.zeros_like(st_ref)
            pf_ref[SLOT_VB, 0:POOL_HALO, :] = jnp.zeros((POOL_HALO, COL_BLK), F32)

        snap_ref[...] = st_ref[...]
        minb_ref[...] = jnp.zeros_like(minb_ref)
        lb_ref[...] = lower_bound()

    @pl.when(is_gate & has_mm & has_mix)
    def _():
        mix_step(with_mm=True)

    @pl.when(is_gate & has_mm & jnp.logical_not(has_mix))
    def _():
        g_ref[...] = _dot(h_ref[...], w_ref[...]).astype(BF16)

    @pl.when(is_gate & jnp.logical_not(has_mm))
    def _():
        mix_step(with_mm=False)

    is_f32_blk = (j == N_GATE_BLK + 1) | (j == N_GATE_BLK + 4)

    @pl.when(jnp.logical_not(is_gate) & has_mm & jnp.logical_not(is_f32_blk))
    def _():
        jj = j - N_GATE_BLK
        slot = jnp.where(jj == 0, SLOT_Q, jnp.where(jj == 5, SLOT_ZB, jj - 1))
        pb_ref[slot] = _dot(h_ref[...], w_ref[...]).astype(BF16)

    @pl.when(jnp.logical_not(is_gate) & has_mm & is_f32_blk)
    def _():
        slot = jnp.where(j == N_GATE_BLK + 1, SLOT_F, SLOT_VB)
        pf_ref[slot, POOL_HALO:POOL_HALO + TM, :] = _dot(h_ref[...], w_ref[...])

    @pl.when(has_mix & (j == N_GATE_BLK - 1))
    def _():
        pf_ref[SLOT_VB, 0:POOL_HALO, :] = pf_ref[SLOT_VB, TM:TM + POOL_HALO, :]

        @pl.when(jnp.min(minb_ref[...]) < -SAFE_CHUNK_LOG_DECAY)
        def _():
            st_ref[...] = snap_ref[...]
            lb = lb_ref[...]
            tri, causal, col = masks()
            hn = hn_ref[...]

            def redo(ci, carry):
                for _ in _hgrn_chunk_stages(
                        pl.multiple_of(ci * CHUNK, CHUNK), lb, 1.0 - lb, hn, tri, causal, col,
                        pb_ref, pf_ref, ya_ref, st_ref, [],
                        direct_refs=(a_ref, bs_ref, qs_ref, ks_ref)):
                    pass
                return carry

            lax.fori_loop(0, TM // CHUNK, redo, 0)


def _proj_mix_call(layer, h, coef, lower_bounds, hgrn_norm, pool_scale, pool_w_b, w_in_b,
                   bsz, seq):
    t = h.shape[0]
    nt = t // TM

    def h_map(i, j):
        return (jnp.minimum(i, nt - 1), 0)

    def w_map(i, j):
        cb = (j + FIRST_GATE_BLK) % N_COL_BLK
        return (layer, 0, jnp.where(i == nt, FIRST_GATE_BLK - 1, cb))

    def g_map(i, j):
        gb = jnp.minimum(j, N_GATE_BLK - 1)
        return (jnp.minimum(i, nt - 1), jnp.where(i == nt, N_GATE_BLK - 1, gb))

    def y_map(i, j):
        return (jnp.maximum(i - 1, 0), 0)

    const3 = lambda i, j: (layer, 0, 0)
    return pl.pallas_call(
        partial(_proj_mix_kernel, seq // TM),
        out_shape=(jax.ShapeDtypeStruct((t, 2 * D_MODEL), BF16),
                   jax.ShapeDtypeStruct((t, A_WIDTH), BF16),
                   jax.ShapeDtypeStruct((t, B_WIDTH), BF16)),
        grid=(nt + 1, N_COL_BLK),
        in_specs=[
            pl.BlockSpec((None, DEPTH, A_WIDTH), const3),
            pl.BlockSpec((DEPTH, A_WIDTH), lambda i, j: (0, 0)),
            pl.BlockSpec((None, 1, A_WIDTH), const3),
            pl.BlockSpec((None, 1, B_WIDTH), const3),
            pl.BlockSpec((None, len(POOL_WINDOWS), POOL_GW, POOL_GW),
                         lambda i, j: (layer, 0, 0, 0)),
            pl.BlockSpec((TM, D_MODEL), h_map),
            pl.BlockSpec((None, D_MODEL, COL_BLK), w_map),
        ],
        out_specs=(pl.BlockSpec((TM, COL_BLK), g_map),
                   pl.BlockSpec((TM, A_WIDTH), y_map),
                   pl.BlockSpec((TM, B_WIDTH), y_map)),
        scratch_shapes=[
            pltpu.VMEM((4, TM, COL_BLK), BF16),
            pltpu.VMEM((2, TM + POOL_HALO, COL_BLK), F32),
            pltpu.VMEM((N_HEADS, HEAD_DIM, HEAD_DIM), F32),
            pltpu.VMEM((N_HEADS, HEAD_DIM, HEAD_DIM), F32),
            pltpu.VMEM((1, A_WIDTH), F32),
            pltpu.VMEM((1, A_WIDTH), F32),
            pltpu.VMEM((N_HEADS, CHUNK, CHUNK), F32),
            pltpu.VMEM((CHUNK, A_WIDTH), F32),
            pltpu.VMEM((CHUNK, A_WIDTH), F32),
            pltpu.VMEM((CHUNK, A_WIDTH), F32),
        ],
        compiler_params=pltpu.CompilerParams(
            dimension_semantics=("arbitrary", "arbitrary"),
            vmem_limit_bytes=VMEM_LIMIT),
        name="proj_mix",
    )(coef, lower_bounds, hgrn_norm.reshape(DEPTH, 1, A_WIDTH),
      pool_scale.reshape(DEPTH, 1, B_WIDTH), pool_w_b, h, w_in_b)


def _out_kernel(emit_h, ya_ref, yb_ref, ga_ref, gb_ref, x_ref, gate_ref, gain_ref,
                wpa_ref, wpb_ref, wo_ref, ngain_ref, nshift_ref, nscale_ref, o_ref, *rest):
    h_ref = rest[0] if emit_h else None
    out_ref = rest[-1]
    i = pl.program_id(0)
    n = pl.num_programs(0) - 1
    n_slices = D_MODEL // OUT_COL_SLICE
    piece = OUT_TM // (2 * n_slices)

    def epilogue_rows(r, gain1, gain2):
        rs = slice(r * piece, (r + 1) * piece)
        out = out_ref[rs, :]
        inv = lax.rsqrt(jnp.mean(out * out, axis=-1, keepdims=True) + EPS)
        xn = x_ref[rs, :] + (out * inv) * gain1
        o_ref[rs, :] = xn
        last = xn
        if emit_h:
            inv2 = lax.rsqrt(jnp.mean(xn * xn, axis=-1, keepdims=True) + EPS)
            last = (xn * inv2) * gain2 + nshift_ref[...]
            h_ref[rs, :] = last.astype(BF16)
        bits = pltpu.bitcast(last, jnp.uint32)
        acc = None
        for rr in range(piece // 8):
            for cc in range(D_MODEL // 128):
                blk = bits[rr * 8:(rr + 1) * 8, cc * 128:(cc + 1) * 128]
                acc = blk if acc is None else acc | blk
        zero = ((acc >> 16) >> 16).astype(F32)[0:1, :]
        return jnp.tile(zero, (1, OUT_COL_SLICE // 128))

    def step(with_matmuls, with_epilogue):
        if with_epilogue:
            gain1 = gate_ref[...] * gain_ref[...]
            gain2 = ngain_ref[...] * (1.0 + nscale_ref[...])
        products = []
        for which, (y_ref, w_ref) in enumerate(((ya_ref, wpa_ref), (yb_ref, wpb_ref))):
            parts = []
            for s in range(n_slices):
                anchor = None
                if with_epilogue:
                    anchor = epilogue_rows(which * n_slices + s, gain1, gain2)
                if with_matmuls:
                    cs = slice(s * OUT_COL_SLICE, (s + 1) * OUT_COL_SLICE)
                    part = _dot(y_ref[...], w_ref[:, cs])
                    parts.append(part if anchor is None else part + anchor)
            products.append(parts)
        if with_matmuls:
            merged = []
            for s in range(n_slices):
                cs = slice(s * OUT_COL_SLICE, (s + 1) * OUT_COL_SLICE)
                merged.append(_sigmoid(ga_ref[:, cs]) * products[0][s].astype(BF16)
                              + _sigmoid(gb_ref[:, cs]) * products[1][s].astype(BF16))
            out_ref[...] = _dot(jnp.concatenate(merged, axis=1), wo_ref[...])

    @pl.when(i == 0)
    def _():
        step(with_matmuls=True, with_epilogue=False)

    @pl.when((i > 0) & (i < n))
    def _():
        step(with_matmuls=True, with_epilogue=True)

    @pl.when(i == n)
    def _():
        step(with_matmuls=False, with_epilogue=True)


def _out_call(layer, ya, yb, g, x2, mod5, norm_post, norm_pre, wpa_b, wpb_b, wo_b, seq):
    t = x2.shape[0]
    n = t // OUT_TM
    tpb = seq // OUT_TM
    emit_h = layer + 1 < DEPTH
    nxt = min(layer + 1, DEPTH - 1)
    resident = dict(pipeline_mode=pl.Buffered(1))

    def cur_spec(width, cb=0):
        return pl.BlockSpec((OUT_TM, width), lambda i: (jnp.minimum(i, n - 1), cb))

    def prev_spec(width):
        return pl.BlockSpec((OUT_TM, width), lambda i: (jnp.maximum(i - 1, 0), 0))

    def prev_mod_spec(lyr, which):
        return pl.BlockSpec((None, None, None, 1, D_MODEL),
                            lambda i: (lyr, jnp.maximum(i - 1, 0) // tpb, which, 0, 0))

    out_shape = [jax.ShapeDtypeStruct((t, D_MODEL), F32)]
    out_specs = [prev_spec(D_MODEL)]
    if emit_h:
        out_shape.append(jax.ShapeDtypeStruct((t, D_MODEL), BF16))
        out_specs.append(prev_spec(D_MODEL))
    res = pl.pallas_call(
        partial(_out_kernel, emit_h),
        out_shape=tuple(out_shape),
        grid=(n + 1,),
        in_specs=[
            cur_spec(A_WIDTH), cur_spec(B_WIDTH),
            cur_spec(D_MODEL, 0), cur_spec(D_MODEL, 1),
            prev_spec(D_MODEL),
            prev_mod_spec(layer, 2),
            pl.BlockSpec((None, 1, D_MODEL), lambda i: (layer, 0, 0)),
            pl.BlockSpec((None, A_WIDTH, D_MODEL), lambda i: (layer, 0, 0), **resident),
            pl.BlockSpec((None, B_WIDTH, D_MODEL), lambda i: (layer, 0, 0), **resident),
            pl.BlockSpec((None, D_MODEL, D_MODEL), lambda i: (layer, 0, 0), **resident),
            pl.BlockSpec((None, 1, D_MODEL), lambda i: (nxt, 0, 0)),
            prev_mod_spec(nxt, 0), prev_mod_spec(nxt, 1),
        ],
        out_specs=tuple(out_specs),
        scratch_shapes=[pltpu.VMEM((OUT_TM, D_MODEL), F32)],
        compiler_params=pltpu.CompilerParams(
            dimension_semantics=("arbitrary",),
            vmem_limit_bytes=VMEM_LIMIT),
        name="out_stage",
    )(ya, yb, g, g, x2, mod5, norm_post.reshape(DEPTH, 1, D_MODEL), wpa_b, wpb_b, wo_b,
      norm_pre.reshape(DEPTH, 1, D_MODEL), mod5, mod5)
    return res if emit_h else (res[0], None)


def kernel(x, c, w_ada, b_ada, norm_pre, norm_post, w_in, lower_bounds, hgrn_norm,
           pool_w, pool_scale, w_proj_a, w_proj_b, w_out):
    bsz, seq, d = x.shape
    assert d == D_MODEL and seq % TM == 0 and seq % OUT_TM == 0 and seq % NORM_TM == 0
    w_in_b = w_in.astype(BF16)
    wpa_b = w_proj_a.astype(BF16)
    wpb_b = w_proj_b.astype(BF16)
    wo_b = w_out.astype(BF16)
    pool_w_b = pool_w.astype(BF16)

    mod = _ada_call(c, w_ada, b_ada)
    mod5 = mod.reshape(DEPTH, bsz, 3, 1, D_MODEL)
    jj = jnp.arange(DEPTH)
    coef = ((jj[None, :] >= 1) & (jj[None, :] <= jj[:, None])).astype(F32)
    coef = jnp.broadcast_to(coef[:, :, None], (DEPTH, DEPTH, A_WIDTH))

    x2 = x.reshape(bsz * seq, D_MODEL)
    h = _norm_call(x2, mod5, norm_pre, seq)
    for layer in range(DEPTH):
        g, ya, yb = _proj_mix_call(layer, h, coef, lower_bounds, hgrn_norm, pool_scale,
                                   pool_w_b, w_in_b, bsz, seq)
        x2, h = _out_call(layer, ya, yb, g, x2, mod5, norm_post, norm_pre,
                          wpa_b, wpb_b, wo_b, seq)
    return x2.reshape(bsz, seq, D_MODEL)
```

```python
from functools import partial

import jax
import jax.numpy as jnp
from jax import lax
from jax.experimental import pallas as pl
from jax.experimental.pallas import tpu as pltpu

D_MODEL = 2048
DEPTH = 4
CHUNK = 64
EPS = 1e-6
MIN_FORGET = 1e-30
HEAD_DIM = 128
A_WIDTH = D_MODEL // 2
N_HEADS = A_WIDTH // HEAD_DIM
B_WIDTH = D_MODEL // 2
POOL_WINDOWS = (2, 4, 8, 16)
POOL_GW = B_WIDTH // len(POOL_WINDOWS)
POOL_HALO = 16
IN_COLS = 6 * A_WIDTH + 2 * D_MODEL
COL_BLK = 1024
N_COL_BLK = IN_COLS // COL_BLK
N_GATE_BLK = 2 * D_MODEL // COL_BLK
FIRST_GATE_BLK = 6 * A_WIDTH // COL_BLK
SLOT_Q, SLOT_VA, SLOT_ZA, SLOT_ZB = 0, 1, 2, 3
SLOT_F, SLOT_VB = 0, 1

SAFE_CHUNK_LOG_DECAY = 120.0

TM = 1024
CAST_ROWS = 16
CHUNKS_PER_STEP = TM // CHUNK // N_GATE_BLK
POOL_ROWS = TM // N_GATE_BLK
OUT_TM = 512
OUT_COL_SLICE = 256
NORM_TM = 1024
VMEM_LIMIT = 60 * 1024 * 1024

F32 = jnp.float32
BF16 = jnp.bfloat16


LOG2_E = 1.4426950408889634


def _sigmoid(v):
    return 0.5 * jnp.tanh(0.5 * v) + 0.5


def _silu(v):
    hv = 0.5 * v
    return hv * jnp.tanh(hv) + hv


def _dot(a, b):
    return jnp.dot(a, b, preferred_element_type=F32)


def _dot_nt(a, b):
    return lax.dot_general(a, b, (((1,), (1,)), ((), ())), preferred_element_type=F32)


def _dot_tn(a, b):
    return lax.dot_general(a, b, (((0,), (0,)), ((), ())), preferred_element_type=F32)


def _prenorm(x, gain, shift, scale):
    inv = lax.rsqrt(jnp.mean(x * x, axis=-1, keepdims=True) + EPS)
    return ((x * inv) * gain) * (1.0 + scale) + shift


def _ada_kernel(ct_ref, w_ref, b_ref, o_ref):
    ct = ct_ref[...]
    ca = _silu(ct)
    w = w_ref[...]
    rows = [jnp.sum(w * ca[:, b:b + 1], axis=0, keepdims=True)
            for b in range(ct.shape[1])]
    o_ref[...] = jnp.concatenate(rows, axis=0) + b_ref[...]


def _ada_call(c, w_ada, b_ada):
    bsz = c.shape[0]
    tn = 1024
    n3 = 3 * D_MODEL
    return pl.pallas_call(
        _ada_kernel,
        out_shape=jax.ShapeDtypeStruct((DEPTH, bsz, n3), F32),
        grid=(DEPTH, n3 // tn),
        in_specs=[
            pl.BlockSpec((D_MODEL, bsz), lambda l, j: (0, 0)),
            pl.BlockSpec((None, D_MODEL, tn), lambda l, j: (l, 0, j)),
            pl.BlockSpec((None, 1, tn), lambda l, j: (l, 0, j)),
        ],
        out_specs=pl.BlockSpec((None, bsz, tn), lambda l, j: (l, 0, j)),
        compiler_params=pltpu.CompilerParams(
            dimension_semantics=("arbitrary", "arbitrary"),
            vmem_limit_bytes=VMEM_LIMIT),
        name="ada_mod",
    )(c.T, w_ada, b_ada.reshape(DEPTH, 1, n3))


def _norm_kernel(x_ref, gain_ref, shift_ref, scale_ref, h_ref):
    h_ref[...] = _prenorm(x_ref[...], gain_ref[...], shift_ref[...], scale_ref[...]).astype(BF16)


def _mod_spec(layer, which, tiles_per_batch):
    return pl.BlockSpec((None, None, None, 1, D_MODEL),
                        lambda i: (layer, i // tiles_per_batch, which, 0, 0))


def _norm_call(x2, mod5, norm_pre, seq):
    t = x2.shape[0]
    tpb = seq // NORM_TM
    return pl.pallas_call(
        _norm_kernel,
        out_shape=jax.ShapeDtypeStruct((t, D_MODEL), BF16),
        grid=(t // NORM_TM,),
        in_specs=[
            pl.BlockSpec((NORM_TM, D_MODEL), lambda i: (i, 0)),
            pl.BlockSpec((None, 1, D_MODEL), lambda i: (0, 0, 0)),
            _mod_spec(0, 0, tpb), _mod_spec(0, 1, tpb),
        ],
        out_specs=pl.BlockSpec((NORM_TM, D_MODEL), lambda i: (i, 0)),
        compiler_params=pltpu.CompilerParams(
            dimension_semantics=("arbitrary",), vmem_limit_bytes=VMEM_LIMIT),
        name="pre_norm",
    )(x2, norm_pre.reshape(DEPTH, 1, D_MODEL), mod5, mod5)


def _split3(v):
    hi = v.astype(BF16)
    r1 = v - hi.astype(F32)
    mid = r1.astype(BF16)
    lo = (r1 - mid.astype(F32)).astype(BF16)
    return hi, mid, lo


def _zero_row(width, arrays):
    total = None
    for arr in arrays:
        bits = pltpu.bitcast(arr, jnp.uint32)
        lanes = min(bits.shape[1], 128)
        acc = None
        for r in range(bits.shape[0] // 8):
            for c in range(bits.shape[1] // lanes):
                blk = bits[r * 8:(r + 1) * 8, c * lanes:(c + 1) * lanes]
                acc = blk if acc is None else acc | blk
        row = jnp.tile(((acc >> 16) >> 16).astype(F32)[0:1, :], (1, width // lanes))
        total = row if total is None else total + row
    return total


def _hgrn_chunk_stages(row0, lb, one_m_lb, hn, tri, causal, col, pb_ref, pf_ref, ya_ref,
                       st_ref, b_last_out, direct_refs=None):
    rows = pl.ds(row0, CHUNK)
    a = pf_ref[SLOT_F, pl.ds(row0 + POOL_HALO, CHUNK), :]
    half_gap = 0.5 * one_m_lb
    gt = half_gap * jnp.tanh(0.5 * a)
    f = (lb + half_gap) + gt
    kk = half_gap - gt
    lf = jnp.log(jnp.maximum(f, MIN_FORGET))
    q = _silu(pb_ref[SLOT_Q, rows, :].astype(F32))
    yield

    hi, mid, lo = _split3(lf)
    yield

    b = _dot(tri, jnp.concatenate([hi, mid, lo], axis=0))
    yield

    b_last = b[CHUNK - 1:CHUNK, :]
    b_last_out.append(b_last)
    st_decay = jnp.exp(b_last)
    if direct_refs is None:
        half = 0.5 * b_last
        d2 = (b - half) * LOG2_E
        q_in = q * jnp.exp2(d2)
        k_in = kk * jnp.exp2(-d2)
        e_half = jnp.exp(half)
        q_st = (q_in * e_half).astype(BF16)
        k_st = (k_in * e_half).astype(BF16)
        q_in = q_in.astype(BF16)
        k_in = k_in.astype(BF16)
    else:
        q_st = (q * jnp.exp(b)).astype(BF16)
        k_st = (kk * jnp.exp(b_last - b)).astype(BF16)
    yield

    heads = [slice(h * HEAD_DIM, (h + 1) * HEAD_DIM) for h in range(N_HEADS)]
    if direct_refs is None:
        scores = [_dot_nt(q_in[:, cs], k_in[:, cs]) for cs in heads]
    else:
        a_ref, bs_ref, qs_ref, ks_ref = direct_refs
        bs_ref[...] = b
        qs_ref[...] = q
        ks_ref[...] = kk
        a_ref[...] = jnp.zeros_like(a_ref)

        def src_body(s, c2):
            brow = bs_ref[pl.ds(s, 1), :]
            krow = ks_ref[pl.ds(s, 1), :]
            p = jnp.exp(jnp.minimum(bs_ref[...] - brow, 0.0)) * qs_ref[...] * krow
            for h, cs in enumerate(heads):
                colv = jnp.sum(p[:, cs], axis=1, keepdims=True)
                a_ref[h] = jnp.where(col == s, colv, a_ref[h])
            return c2

        lax.fori_loop(0, CHUNK, src_body, 0)
        scores = [a_ref[h] for h in range(N_HEADS)]
    yield

    scores = [jnp.where(causal, sc, 0.0).astype(BF16) for sc in scores]
    yield

    outs, updates = [], []
    for h, cs in enumerate(heads):
        v = pb_ref[SLOT_VA, rows, cs]
        outs.append(_dot_nt(q_st[:, cs], st_ref[h].astype(BF16)) + _dot(scores[h], v))
        updates.append(_dot_tn(v, k_st[:, cs]))
    yield

    for h, cs in enumerate(heads):
        if h == N_HEADS // 2:
            yield
        st_ref[h] = st_ref[h] * st_decay[:, cs] + updates[h]
        o = outs[h]
        o = o * lax.rsqrt(jnp.mean(o * o, axis=-1, keepdims=True) + EPS)
        z = pb_ref[SLOT_ZA, rows, cs].astype(F32)
        ya_ref[rows, cs] = (o * hn[:, cs] * _silu(z)).astype(BF16)


def _pool_group_stages(g, row0, seq_row0, pb_ref, pf_ref, pw_ref, pscale_ref, yb_ref):
    w = POOL_WINDOWS[g]
    n = POOL_ROWS + POOL_HALO
    pos = (seq_row0 + row0 + lax.broadcasted_iota(jnp.int32, (POOL_ROWS, POOL_GW), 0) + 1
           ).astype(F32)
    rows = pl.ds(row0, POOL_ROWS)
    cs = slice(g * POOL_GW, (g + 1) * POOL_GW)
    ext = pf_ref[SLOT_VB, pl.ds(row0, n), cs]
    tot = ext
    d = 1
    while d < w:
        tot = tot + pltpu.roll(tot, d, axis=0)
        d *= 2
    cur = ext[POOL_HALO:, :]
    pooled = (tot[POOL_HALO:, :] / jnp.minimum(pos, float(w)) - cur).astype(BF16)
    yield

    mixed = _dot(pooled, pw_ref[g])
    yield

    z = pb_ref[SLOT_ZB, rows, cs].astype(F32)
    yb_ref[rows, cs] = (mixed * pscale_ref[:, cs] * _silu(z)).astype(BF16)


FUSED_SLOTS = (
    (0, 0),
    (0, 1, 1),
    (1, 0, 'p0'),
    (0, 'p0', 1, 2, 2),
    (1, 2, 0, 'p0', 'p1'),
    (0, 'p1', 1, 2, 3),
    (2, 0, 0, 3, 'p1'),
    (1, 3, 2, 'p2'),
    ('p2', 1, 1, 3),
    (2, 3, 'p2', 'p3'),
    ('p3', 2, 2, 3),
    (3, 'p3'),
    (3, 3),
)
MM_ROW_SPLIT = 4
MM_COL_SPLIT = 4


def _proj_mix_kernel(tiles_per_row, cast_next, coef_ref, lbraw_ref, hn_ref, pscale_ref, pw_ref,
                     h_ref, w_ref, *refs):
    if cast_next:
        wnext_f32_ref, g_ref, ya_ref, yb_ref, wnext_b_ref = refs[:5]
        wnext_b_ref[...] = wnext_f32_ref[...].astype(BF16)
        refs = refs[5:]
    else:
        g_ref, ya_ref, yb_ref = refs[:3]
        refs = refs[3:]
    pb_ref, pf_ref, st_ref, snap_ref, minb_ref, lb_ref, a_ref, bs_ref, qs_ref, ks_ref = refs
    i = pl.program_id(0)
    j = pl.program_id(1)
    nt = pl.num_programs(0) - 1
    row_tile = (i - 1) % tiles_per_row
    has_mm = i < nt
    has_mix = i > 0
    is_gate = j < N_GATE_BLK

    def lower_bound():
        lbr = lbraw_ref[...]
        e = jnp.exp(lbr - jnp.max(lbr, axis=0, keepdims=True))
        sm = e / jnp.sum(e, axis=0, keepdims=True)
        return jnp.sum(coef_ref[...] * sm, axis=0, keepdims=True)

    def masks():
        row = lax.broadcasted_iota(jnp.int32, (CHUNK, CHUNK), 0)
        col = lax.broadcasted_iota(jnp.int32, (CHUNK, CHUNK), 1)
        causal = row >= col
        tri = causal.astype(BF16)
        return jnp.concatenate([tri, tri, tri], axis=1), causal, col

    mb, nb = TM // MM_ROW_SPLIT, COL_BLK // MM_COL_SPLIT

    def mm_slice(k):
        r, c = k % MM_ROW_SPLIT, k // MM_ROW_SPLIT
        rs, cs = slice(r * mb, (r + 1) * mb), slice(c * nb, (c + 1) * nb)
        g_ref[rs, cs] = _dot(h_ref[rs, :], w_ref[:, cs]).astype(BF16)

    def mix_step(with_mm):
        lb = lb_ref[...]
        tri, causal, col = masks()
        hn = hn_ref[...]
        b_lasts = []
        chunks = [
            _hgrn_chunk_stages(pl.multiple_of((j * CHUNKS_PER_STEP + cc) * CHUNK, CHUNK),
                               lb, 1.0 - lb, hn, tri, causal, col, pb_ref, pf_ref, ya_ref,
                               st_ref, b_lasts)
            for cc in range(CHUNKS_PER_STEP)]
        pools = {
            'p%d' % g: _pool_group_stages(g, pl.multiple_of(j * POOL_ROWS, POOL_ROWS),
                                          row_tile * TM, pb_ref, pf_ref, pw_ref, pscale_ref,
                                          yb_ref)
            for g in range(len(POOL_WINDOWS))}
        n_slices = MM_ROW_SPLIT * MM_COL_SPLIT
        assert len(FUSED_SLOTS) <= n_slices
        for k in range(n_slices):
            for item in (FUSED_SLOTS[k] if k < len(FUSED_SLOTS) else ()):
                next(pools[item] if isinstance(item, str) else chunks[item], None)
            if with_mm:
                mm_slice(k)
        done = object()
        assert all(next(gen, done) is done for gen in [*chunks, *pools.values()])
        minb = minb_ref[...]
        for b_last in b_lasts:
            minb = jnp.minimum(minb, b_last)
        minb_ref[...] = minb

    @pl.when(has_mix & (j == 0))
    def _():
        @pl.when(row_tile == 0)
        def _():
            st_ref[...] = jnp.zeros_like(st_ref)
            pf_ref[SLOT_VB, 0:POOL_HALO, :] = jnp.zeros((POOL_HALO, COL_BLK), F32)

        snap_ref[...] = st_ref[...]
        minb_ref[...] = jnp.zeros_like(minb_ref)
        lb_ref[...] = lower_bound()

    @pl.when(is_gate & has_mm & has_mix)
    def _():
        mix_step(with_mm=True)

    @pl.when(is_gate & has_mm & jnp.logical_not(has_mix))
    def _():
        g_ref[...] = _dot(h_ref[...], w_ref[...]).astype(BF16)

    @pl.when(is_gate & jnp.logical_not(has_mm))
    def _():
        mix_step(with_mm=False)

    is_f32_blk = (j == N_GATE_BLK + 1) | (j == N_GATE_BLK + 4)

    @pl.when(jnp.logical_not(is_gate) & has_mm & jnp.logical_not(is_f32_blk))
    def _():
        jj = j - N_GATE_BLK
        slot = jnp.where(jj == 0, SLOT_Q, jnp.where(jj == 5, SLOT_ZB, jj - 1))
        pb_ref[slot] = _dot(h_ref[...], w_ref[...]).astype(BF16)

    @pl.when(jnp.logical_not(is_gate) & has_mm & is_f32_blk)
    def _():
        slot = jnp.where(j == N_GATE_BLK + 1, SLOT_F, SLOT_VB)
        pf_ref[slot, POOL_HALO:POOL_HALO + TM, :] = _dot(h_ref[...], w_ref[...])

    @pl.when(has_mix & (j == N_GATE_BLK - 1))
    def _():
        pf_ref[SLOT_VB, 0:POOL_HALO, :] = pf_ref[SLOT_VB, TM:TM + POOL_HALO, :]

        @pl.when(jnp.min(minb_ref[...]) < -SAFE_CHUNK_LOG_DECAY)
        def _():
            st_ref[...] = snap_ref[...]
            lb = lb_ref[...]
            tri, causal, col = masks()
            hn = hn_ref[...]

            def redo(ci, carry):
                for _ in _hgrn_chunk_stages(
                        pl.multiple_of(ci * CHUNK, CHUNK), lb, 1.0 - lb, hn, tri, causal, col,
                        pb_ref, pf_ref, ya_ref, st_ref, [],
                        direct_refs=(a_ref, bs_ref, qs_ref, ks_ref)):
                    pass
                return carry

            lax.fori_loop(0, TM // CHUNK, redo, 0)


def _proj_mix_call(layer, h, coef, lower_bounds, hgrn_norm, pool_scale, pool_w_b, w_b,
                   w_in_f32, seq):
    t = h.shape[0]
    nt = t // TM
    cast_next = layer + 1 < DEPTH
    n_cast_blk = D_MODEL // CAST_ROWS
    assert n_cast_blk <= nt * N_COL_BLK

    def h_map(i, j):
        return (jnp.minimum(i, nt - 1), 0)

    def w_map(i, j):
        cb = (j + FIRST_GATE_BLK) % N_COL_BLK
        return (0, jnp.where(i == nt, FIRST_GATE_BLK - 1, cb))

    def cast_row(i, j):
        return jnp.minimum(i * N_COL_BLK + j, n_cast_blk - 1)

    def g_map(i, j):
        gb = jnp.minimum(j, N_GATE_BLK - 1)
        return (jnp.minimum(i, nt - 1), jnp.where(i == nt, N_GATE_BLK - 1, gb))

    def y_map(i, j):
        return (jnp.maximum(i - 1, 0), 0)

    const3 = lambda i, j: (layer, 0, 0)
    in_specs = [
        pl.BlockSpec((None, DEPTH, A_WIDTH), const3),
        pl.BlockSpec((DEPTH, A_WIDTH), lambda i, j: (0, 0)),
        pl.BlockSpec((None, 1, A_WIDTH), const3),
        pl.BlockSpec((None, 1, B_WIDTH), const3),
        pl.BlockSpec((None, len(POOL_WINDOWS), POOL_GW, POOL_GW),
                     lambda i, j: (layer, 0, 0, 0)),
        pl.BlockSpec((TM, D_MODEL), h_map),
        pl.BlockSpec((D_MODEL, COL_BLK), w_map),
    ]
    out_shape = [jax.ShapeDtypeStruct((t, 2 * D_MODEL), BF16),
                 jax.ShapeDtypeStruct((t, A_WIDTH), BF16),
                 jax.ShapeDtypeStruct((t, B_WIDTH), BF16)]
    out_specs = [pl.BlockSpec((TM, COL_BLK), g_map),
                 pl.BlockSpec((TM, A_WIDTH), y_map),
                 pl.BlockSpec((TM, B_WIDTH), y_map)]
    operands = [coef, lower_bounds, hgrn_norm.reshape(DEPTH, 1, A_WIDTH),
                pool_scale.reshape(DEPTH, 1, B_WIDTH), pool_w_b, h, w_b]
    if cast_next:
        in_specs.append(pl.BlockSpec((None, CAST_ROWS, IN_COLS),
                                     lambda i, j: (layer + 1, cast_row(i, j), 0)))
        out_shape.append(jax.ShapeDtypeStruct((D_MODEL, IN_COLS), BF16))
        out_specs.append(pl.BlockSpec((CAST_ROWS, IN_COLS), lambda i, j: (cast_row(i, j), 0)))
        operands.append(w_in_f32)
    return pl.pallas_call(
        partial(_proj_mix_kernel, seq // TM, cast_next),
        out_shape=tuple(out_shape),
        grid=(nt + 1, N_COL_BLK),
        in_specs=in_specs,
        out_specs=tuple(out_specs),
        scratch_shapes=[
            pltpu.VMEM((4, TM, COL_BLK), BF16),
            pltpu.VMEM((2, TM + POOL_HALO, COL_BLK), F32),
            pltpu.VMEM((N_HEADS, HEAD_DIM, HEAD_DIM), F32),
            pltpu.VMEM((N_HEADS, HEAD_DIM, HEAD_DIM), F32),
            pltpu.VMEM((1, A_WIDTH), F32),
            pltpu.VMEM((1, A_WIDTH), F32),
            pltpu.VMEM((N_HEADS, CHUNK, CHUNK), F32),
            pltpu.VMEM((CHUNK, A_WIDTH), F32),
            pltpu.VMEM((CHUNK, A_WIDTH), F32),
            pltpu.VMEM((CHUNK, A_WIDTH), F32),
        ],
        compiler_params=pltpu.CompilerParams(
            dimension_semantics=("arbitrary", "arbitrary"),
            vmem_limit_bytes=VMEM_LIMIT),
        name="proj_mix",
    )(*operands)


def _out_kernel(emit_h, ya_ref, yb_ref, ga_ref, gb_ref, x_ref, gate_ref, gain_ref,
                wpa_ref, wpb_ref, wo_ref, ngain_ref, nshift_ref, nscale_ref, o_ref, *rest):
    h_ref = rest[0] if emit_h else None
    out_ref = rest[-1]
    i = pl.program_id(0)
    n = pl.num_programs(0) - 1
    n_slices = D_MODEL // OUT_COL_SLICE
    piece = OUT_TM // (2 * n_slices)

    def epilogue_rows(r, gain1, gain2):
        rs = slice(r * piece, (r + 1) * piece)
        out = out_ref[rs, :]
        inv = lax.rsqrt(jnp.mean(out * out, axis=-1, keepdims=True) + EPS)
        xn = x_ref[rs, :] + (out * inv) * gain1
        o_ref[rs, :] = xn
        last = xn
        if emit_h:
            inv2 = lax.rsqrt(jnp.mean(xn * xn, axis=-1, keepdims=True) + EPS)
            last = (xn * inv2) * gain2 + nshift_ref[...]
            h_ref[rs, :] = last.astype(BF16)
        return _zero_row(OUT_COL_SLICE, [last])

    def step(with_matmuls, with_epilogue):
        if with_epilogue:
            gain1 = gate_ref[...] * gain_ref[...]
            gain2 = ngain_ref[...] * (1.0 + nscale_ref[...])
        products = []
        for which, (y_ref, w_ref) in enumerate(((ya_ref, wpa_ref), (yb_ref, wpb_ref))):
            parts = []
            for s in range(n_slices):
                anchor = None
                if with_epilogue:
                    anchor = epilogue_rows(which * n_slices + s, gain1, gain2)
                if with_matmuls:
                    cs = slice(s * OUT_COL_SLICE, (s + 1) * OUT_COL_SLICE)
                    part = _dot(y_ref[...], w_ref[:, cs])
                    parts.append(part if anchor is None else part + anchor)
            products.append(parts)
        if with_matmuls:
            merged = []
            for s in range(n_slices):
                cs = slice(s * OUT_COL_SLICE, (s + 1) * OUT_COL_SLICE)
                merged.append(_sigmoid(ga_ref[:, cs]) * products[0][s].astype(BF16)
                              + _sigmoid(gb_ref[:, cs]) * products[1][s].astype(BF16))
            out_ref[...] = _dot(jnp.concatenate(merged, axis=1), wo_ref[...])

    @pl.when(i == 0)
    def _():
        step(with_matmuls=True, with_epilogue=False)

    @pl.when((i > 0) & (i < n))
    def _():
        step(with_matmuls=True, with_epilogue=True)

    @pl.when(i == n)
    def _():
        step(with_matmuls=False, with_epilogue=True)


def _out_call(layer, ya, yb, g, x2, mod5, norm_post, norm_pre, wpa_b, wpb_b, wo_b, seq):
    t = x2.shape[0]
    n = t // OUT_TM
    tpb = seq // OUT_TM
    emit_h = layer + 1 < DEPTH
    nxt = min(layer + 1, DEPTH - 1)
    resident = dict(pipeline_mode=pl.Buffered(1))

    def cur_spec(width, cb=0):
        return pl.BlockSpec((OUT_TM, width), lambda i: (jnp.minimum(i, n - 1), cb))

    def prev_spec(width):
        return pl.BlockSpec((OUT_TM, width), lambda i: (jnp.maximum(i - 1, 0), 0))

    def prev_mod_spec(lyr, which):
        return pl.BlockSpec((None, None, None, 1, D_MODEL),
                            lambda i: (lyr, jnp.maximum(i - 1, 0) // tpb, which, 0, 0))

    out_shape = [jax.ShapeDtypeStruct((t, D_MODEL), F32)]
    out_specs = [prev_spec(D_MODEL)]
    if emit_h:
        out_shape.append(jax.ShapeDtypeStruct((t, D_MODEL), BF16))
        out_specs.append(prev_spec(D_MODEL))
    res = pl.pallas_call(
        partial(_out_kernel, emit_h),
        out_shape=tuple(out_shape),
        grid=(n + 1,),
        in_specs=[
            cur_spec(A_WIDTH), cur_spec(B_WIDTH),
            cur_spec(D_MODEL, 0), cur_spec(D_MODEL, 1),
            prev_spec(D_MODEL),
            prev_mod_spec(layer, 2),
            pl.BlockSpec((None, 1, D_MODEL), lambda i: (layer, 0, 0)),
            pl.BlockSpec((None, A_WIDTH, D_MODEL), lambda i: (layer, 0, 0), **resident),
            pl.BlockSpec((None, B_WIDTH, D_MODEL), lambda i: (layer, 0, 0), **resident),
            pl.BlockSpec((None, D_MODEL, D_MODEL), lambda i: (layer, 0, 0), **resident),
            pl.BlockSpec((None, 1, D_MODEL), lambda i: (nxt, 0, 0)),
            prev_mod_spec(nxt, 0), prev_mod_spec(nxt, 1),
        ],
        out_specs=tuple(out_specs),
        scratch_shapes=[pltpu.VMEM((OUT_TM, D_MODEL), F32)],
        compiler_params=pltpu.CompilerParams(
            dimension_semantics=("arbitrary",),
            vmem_limit_bytes=VMEM_LIMIT),
        name="out_stage",
    )(ya, yb, g, g, x2, mod5, norm_post.reshape(DEPTH, 1, D_MODEL), wpa_b, wpb_b, wo_b,
      norm_pre.reshape(DEPTH, 1, D_MODEL), mod5, mod5)
    return res if emit_h else (res[0], None)


def kernel(x, c, w_ada, b_ada, norm_pre, norm_post, w_in, lower_bounds, hgrn_norm,
           pool_w, pool_scale, w_proj_a, w_proj_b, w_out):
    bsz, seq, d = x.shape
    assert d == D_MODEL and seq % TM == 0 and seq % OUT_TM == 0 and seq % NORM_TM == 0
    wpa_b = w_proj_a.astype(BF16)
    wpb_b = w_proj_b.astype(BF16)
    wo_b = w_out.astype(BF16)
    pool_w_b = pool_w.astype(BF16)

    mod = _ada_call(c, w_ada, b_ada)
    mod5 = mod.reshape(DEPTH, bsz, 3, 1, D_MODEL)
    jj = jnp.arange(DEPTH)
    coef = ((jj[None, :] >= 1) & (jj[None, :] <= jj[:, None])).astype(F32)
    coef = jnp.broadcast_to(coef[:, :, None], (DEPTH, DEPTH, A_WIDTH))

    x2 = x.reshape(bsz * seq, D_MODEL)
    h = _norm_call(x2, mod5, norm_pre, seq)
    w_b = w_in[0].astype(BF16)
    for layer in range(DEPTH):
        g, ya, yb, *w_next = _proj_mix_call(layer, h, coef, lower_bounds, hgrn_norm, pool_scale,
                                            pool_w_b, w_b, w_in, seq)
        if w_next:
            w_b = w_next[0]
        x2, h = _out_call(layer, ya, yb, g, x2, mod5, norm_post, norm_pre,
                          wpa_b, wpb_b, wo_b, seq)
    return x2.reshape(bsz, seq, D_MODEL)
```

```python
from functools import partial

import jax
import jax.numpy as jnp
from jax import lax
from jax.experimental import pallas as pl
from jax.experimental.pallas import tpu as pltpu

D_MODEL = 2048
DEPTH = 4
CHUNK = 64
EPS = 1e-6
MIN_FORGET = 1e-30
HEAD_DIM = 128
A_WIDTH = D_MODEL // 2
N_HEADS = A_WIDTH // HEAD_DIM
B_WIDTH = D_MODEL // 2
POOL_WINDOWS = (2, 4, 8, 16)
POOL_GW = B_WIDTH // len(POOL_WINDOWS)
POOL_HALO = 16
IN_COLS = 6 * A_WIDTH + 2 * D_MODEL
COL_BLK = 1024
N_COL_BLK = IN_COLS // COL_BLK
N_GATE_BLK = 2 * D_MODEL // COL_BLK
FIRST_GATE_BLK = 6 * A_WIDTH // COL_BLK
SLOT_Q, SLOT_VA, SLOT_ZA, SLOT_ZB = 0, 1, 2, 3
SLOT_F, SLOT_VB = 0, 1

SAFE_CHUNK_LOG_DECAY = 120.0

TM = 1024
CAST_ROWS = 16
CHUNKS_PER_STEP = TM // CHUNK // N_GATE_BLK
POOL_ROWS = TM // N_GATE_BLK
OUT_TM = 512
OUT_COL_SLICE = 256
EPI_GATE, EPI_POST_GAIN, EPI_NEXT_GAIN, EPI_NEXT_SHIFT, EPI_NEXT_SCALE = range(5)
N_EPI_ROWS = 5
NORM_TM = 1024
ADA_TN = 2048
VMEM_LIMIT = 60 * 1024 * 1024

F32 = jnp.float32
BF16 = jnp.bfloat16


LOG2_E = 1.4426950408889634


def _sigmoid(v):
    return 0.5 * jnp.tanh(0.5 * v) + 0.5


def _silu(v):
    hv = 0.5 * v
    return hv * jnp.tanh(hv) + hv


def _dot(a, b):
    return jnp.dot(a, b, preferred_element_type=F32)


def _dot_nt(a, b):
    return lax.dot_general(a, b, (((1,), (1,)), ((), ())), preferred_element_type=F32)


def _dot_tn(a, b):
    return lax.dot_general(a, b, (((0,), (0,)), ((), ())), preferred_element_type=F32)


def _prenorm(x, gain, shift, scale):
    inv = lax.rsqrt(jnp.mean(x * x, axis=-1, keepdims=True) + EPS)
    return ((x * inv) * gain) * (1.0 + scale) + shift


def _ada_kernel(ct_ref, w_ref, b_ref, o_ref):
    ct = ct_ref[...]
    ca = _silu(ct)
    w = w_ref[...]
    rows = [jnp.sum(w * ca[:, b:b + 1], axis=0, keepdims=True)
            for b in range(ct.shape[1])]
    o_ref[...] = jnp.concatenate(rows, axis=0) + b_ref[...]


def _ada_call(c, w_ada, b_ada):
    bsz = c.shape[0]
    tn = ADA_TN
    n3 = 3 * D_MODEL
    return pl.pallas_call(
        _ada_kernel,
        out_shape=jax.ShapeDtypeStruct((DEPTH, bsz, n3), F32),
        grid=(DEPTH, n3 // tn),
        in_specs=[
            pl.BlockSpec((D_MODEL, bsz), lambda l, j: (0, 0)),
            pl.BlockSpec((None, D_MODEL, tn), lambda l, j: (l, 0, j)),
            pl.BlockSpec((None, 1, tn), lambda l, j: (l, 0, j)),
        ],
        out_specs=pl.BlockSpec((None, bsz, tn), lambda l, j: (l, 0, j)),
        compiler_params=pltpu.CompilerParams(
            dimension_semantics=("arbitrary", "arbitrary"),
            vmem_limit_bytes=VMEM_LIMIT),
        name="ada_mod",
    )(c.T, w_ada, b_ada.reshape(DEPTH, 1, n3))


def _norm_kernel(x_ref, gain_ref, shift_ref, scale_ref, h_ref):
    h_ref[...] = _prenorm(x_ref[...], gain_ref[...], shift_ref[...], scale_ref[...]).astype(BF16)


def _mod_spec(layer, which, tiles_per_batch):
    return pl.BlockSpec((None, None, None, 1, D_MODEL),
                        lambda i: (layer, i // tiles_per_batch, which, 0, 0))


def _norm_call(x2, mod5, norm_pre, seq):
    t = x2.shape[0]
    tpb = seq // NORM_TM
    return pl.pallas_call(
        _norm_kernel,
        out_shape=jax.ShapeDtypeStruct((t, D_MODEL), BF16),
        grid=(t // NORM_TM,),
        in_specs=[
            pl.BlockSpec((NORM_TM, D_MODEL), lambda i: (i, 0)),
            pl.BlockSpec((None, 1, D_MODEL), lambda i: (0, 0, 0)),
            _mod_spec(0, 0, tpb), _mod_spec(0, 1, tpb),
        ],
        out_specs=pl.BlockSpec((NORM_TM, D_MODEL), lambda i: (i, 0)),
        compiler_params=pltpu.CompilerParams(
            dimension_semantics=("arbitrary",), vmem_limit_bytes=VMEM_LIMIT),
        name="pre_norm",
    )(x2, norm_pre.reshape(DEPTH, 1, D_MODEL), mod5, mod5)


def _split3(v):
    hi = v.astype(BF16)
    r1 = v - hi.astype(F32)
    mid = r1.astype(BF16)
    lo = (r1 - mid.astype(F32)).astype(BF16)
    return hi, mid, lo


def _zero_row(width, arrays):
    total = None
    for arr in arrays:
        bits = pltpu.bitcast(arr, jnp.uint32)
        lanes = min(bits.shape[1], 128)
        acc = None
        for r in range(bits.shape[0] // 8):
            for c in range(bits.shape[1] // lanes):
                blk = bits[r * 8:(r + 1) * 8, c * lanes:(c + 1) * lanes]
                acc = blk if acc is None else acc | blk
        row = jnp.tile(((acc >> 16) >> 16).astype(F32)[0:1, :], (1, width // lanes))
        total = row if total is None else total + row
    return total


def _hgrn_chunk_stages(row0, lb, one_m_lb, hn, tri, causal, col, pb_ref, pf_ref, ya_ref,
                       st_ref, b_last_out, direct_refs=None):
    rows = pl.ds(row0, CHUNK)
    a = pf_ref[SLOT_F, pl.ds(row0 + POOL_HALO, CHUNK), :]
    half_gap = 0.5 * one_m_lb
    gt = half_gap * jnp.tanh(0.5 * a)
    f = (lb + half_gap) + gt
    kk = half_gap - gt
    lf = jnp.log(jnp.maximum(f, MIN_FORGET))
    q = _silu(pb_ref[SLOT_Q, rows, :].astype(F32))
    yield

    hi, mid, lo = _split3(lf)
    yield

    b = _dot(tri, jnp.concatenate([hi, mid, lo], axis=0))
    yield

    b_last = b[CHUNK - 1:CHUNK, :]
    b_last_out.append(b_last)
    st_decay = jnp.exp(b_last)
    if direct_refs is None:
        half = 0.5 * b_last
        d2 = (b - half) * LOG2_E
        q_in = q * jnp.exp2(d2)
        k_in = kk * jnp.exp2(-d2)
        e_half = jnp.exp(half)
        q_st = (q_in * e_half).astype(BF16)
        k_st = (k_in * e_half).astype(BF16)
        q_in = q_in.astype(BF16)
        k_in = k_in.astype(BF16)
    else:
        q_st = (q * jnp.exp(b)).astype(BF16)
        k_st = (kk * jnp.exp(b_last - b)).astype(BF16)
    yield

    heads = [slice(h * HEAD_DIM, (h + 1) * HEAD_DIM) for h in range(N_HEADS)]
    if direct_refs is None:
        scores = [_dot_nt(q_in[:, cs], k_in[:, cs]) for cs in heads]
    else:
        a_ref, bs_ref, qs_ref, ks_ref = direct_refs
        bs_ref[...] = b
        qs_ref[...] = q
        ks_ref[...] = kk
        a_ref[...] = jnp.zeros_like(a_ref)

        def src_body(s, c2):
            brow = bs_ref[pl.ds(s, 1), :]
            krow = ks_ref[pl.ds(s, 1), :]
            p = jnp.exp(jnp.minimum(bs_ref[...] - brow, 0.0)) * qs_ref[...] * krow
            for h, cs in enumerate(heads):
                colv = jnp.sum(p[:, cs], axis=1, keepdims=True)
                a_ref[h] = jnp.where(col == s, colv, a_ref[h])
            return c2

        lax.fori_loop(0, CHUNK, src_body, 0)
        scores = [a_ref[h] for h in range(N_HEADS)]
    yield

    scores = [jnp.where(causal, sc, 0.0).astype(BF16) for sc in scores]
    yield

    outs, updates = [], []
    for h, cs in enumerate(heads):
        v = pb_ref[SLOT_VA, rows, cs]
        outs.append(_dot_nt(q_st[:, cs], st_ref[h].astype(BF16)) + _dot(scores[h], v))
        updates.append(_dot_tn(v, k_st[:, cs]))
    yield

    for h, cs in enumerate(heads):
        if h == N_HEADS // 2:
            yield
        st_ref[h] = st_ref[h] * st_decay[:, cs] + updates[h]
        o = outs[h]
        o = o * lax.rsqrt(jnp.mean(o * o, axis=-1, keepdims=True) + EPS)
        z = pb_ref[SLOT_ZA, rows, cs].astype(F32)
        ya_ref[rows, cs] = (o * hn[:, cs] * _silu(z)).astype(BF16)


def _pool_group_stages(g, row0, seq_row0, pb_ref, pf_ref, pw_ref, pscale_ref, yb_ref):
    w = POOL_WINDOWS[g]
    n = POOL_ROWS + POOL_HALO
    pos = (seq_row0 + row0 + lax.broadcasted_iota(jnp.int32, (POOL_ROWS, POOL_GW), 0) + 1
           ).astype(F32)
    rows = pl.ds(row0, POOL_ROWS)
    cs = slice(g * POOL_GW, (g + 1) * POOL_GW)
    ext = pf_ref[SLOT_VB, pl.ds(row0, n), cs]
    tot = ext
    d = 1
    while d < w:
        tot = tot + pltpu.roll(tot, d, axis=0)
        d *= 2
    cur = ext[POOL_HALO:, :]
    pooled = (tot[POOL_HALO:, :] / jnp.minimum(pos, float(w)) - cur).astype(BF16)
    yield

    mixed = _dot(pooled, pw_ref[g])
    yield

    z = pb_ref[SLOT_ZB, rows, cs].astype(F32)
    yb_ref[rows, cs] = (mixed * pscale_ref[:, cs] * _silu(z)).astype(BF16)


FUSED_SLOTS = (
    (0, 0),
    (0, 1, 1),
    (1, 0, 'p0'),
    (0, 'p0', 1, 2, 2),
    (1, 2, 0, 'p0', 'p1'),
    (0, 'p1', 1, 2, 3),
    (2, 0, 0, 3, 'p1'),
    (1, 3, 2, 'p2'),
    ('p2', 1, 1, 3),
    (2, 3, 'p2', 'p3'),
    ('p3', 2, 2, 3),
    (3, 'p3'),
    (3, 3),
)
MM_ROW_SPLIT = 4
MM_COL_SPLIT = 4


def _proj_mix_kernel(tiles_per_row, cast_next, coef_ref, lbraw_ref, hn_ref, pscale_ref, pw_ref,
                     h_ref, w_ref, *refs):
    if cast_next:
        wnext_f32_ref, g_ref, ya_ref, yb_ref, wnext_b_ref = refs[:5]
        wnext_b_ref[...] = wnext_f32_ref[...].astype(BF16)
        refs = refs[5:]
    else:
        g_ref, ya_ref, yb_ref = refs[:3]
        refs = refs[3:]
    pb_ref, pf_ref, st_ref, snap_ref, minb_ref, lb_ref, a_ref, bs_ref, qs_ref, ks_ref = refs
    i = pl.program_id(0)
    j = pl.program_id(1)
    nt = pl.num_programs(0) - 1
    row_tile = (i - 1) % tiles_per_row
    has_mm = i < nt
    has_mix = i > 0
    is_gate = j < N_GATE_BLK

    def lower_bound():
        lbr = lbraw_ref[...]
        e = jnp.exp(lbr - jnp.max(lbr, axis=0, keepdims=True))
        sm = e / jnp.sum(e, axis=0, keepdims=True)
        return jnp.sum(coef_ref[...] * sm, axis=0, keepdims=True)

    def masks():
        row = lax.broadcasted_iota(jnp.int32, (CHUNK, CHUNK), 0)
        col = lax.broadcasted_iota(jnp.int32, (CHUNK, CHUNK), 1)
        causal = row >= col
        tri = causal.astype(BF16)
        return jnp.concatenate([tri, tri, tri], axis=1), causal, col

    mb, nb = TM // MM_ROW_SPLIT, COL_BLK // MM_COL_SPLIT

    def mm_slice(k):
        r, c = k % MM_ROW_SPLIT, k // MM_ROW_SPLIT
        rs, cs = slice(r * mb, (r + 1) * mb), slice(c * nb, (c + 1) * nb)
        g_ref[rs, cs] = _dot(h_ref[rs, :], w_ref[:, cs]).astype(BF16)

    def mix_step(with_mm):
        lb = lb_ref[...]
        tri, causal, col = masks()
        hn = hn_ref[...]
        b_lasts = []
        chunks = [
            _hgrn_chunk_stages(pl.multiple_of((j * CHUNKS_PER_STEP + cc) * CHUNK, CHUNK),
                               lb, 1.0 - lb, hn, tri, causal, col, pb_ref, pf_ref, ya_ref,
                               st_ref, b_lasts)
            for cc in range(CHUNKS_PER_STEP)]
        pools = {
            'p%d' % g: _pool_group_stages(g, pl.multiple_of(j * POOL_ROWS, POOL_ROWS),
                                          row_tile * TM, pb_ref, pf_ref, pw_ref, pscale_ref,
                                          yb_ref)
            for g in range(len(POOL_WINDOWS))}
        n_slices = MM_ROW_SPLIT * MM_COL_SPLIT
        assert len(FUSED_SLOTS) <= n_slices
        for k in range(n_slices):
            for item in (FUSED_SLOTS[k] if k < len(FUSED_SLOTS) else ()):
                next(pools[item] if isinstance(item, str) else chunks[item], None)
            if with_mm:
                mm_slice(k)
        done = object()
        assert all(next(gen, done) is done for gen in [*chunks, *pools.values()])
        minb = minb_ref[...]
        for b_last in b_lasts:
            minb = jnp.minimum(minb, b_last)
        minb_ref[...] = minb

    @pl.when(has_mix & (j == 0))
    def _():
        @pl.when(row_tile == 0)
        def _():
            st_ref[...] = jnp.zeros_like(st_ref)
            pf_ref[SLOT_VB, 0:POOL_HALO, :] = jnp.zeros((POOL_HALO, COL_BLK), F32)

        snap_ref[...] = st_ref[...]
        minb_ref[...] = jnp.zeros_like(minb_ref)
        lb_ref[...] = lower_bound()

    @pl.when(is_gate & has_mm & has_mix)
    def _():
        mix_step(with_mm=True)

    @pl.when(is_gate & has_mm & jnp.logical_not(has_mix))
    def _():
        g_ref[...] = _dot(h_ref[...], w_ref[...]).astype(BF16)

    @pl.when(is_gate & jnp.logical_not(has_mm))
    def _():
        mix_step(with_mm=False)

    is_f32_blk = (j == N_GATE_BLK + 1) | (j == N_GATE_BLK + 4)

    @pl.when(jnp.logical_not(is_gate) & has_mm & jnp.logical_not(is_f32_blk))
    def _():
        jj = j - N_GATE_BLK
        slot = jnp.where(jj == 0, SLOT_Q, jnp.where(jj == 5, SLOT_ZB, jj - 1))
        pb_ref[slot] = _dot(h_ref[...], w_ref[...]).astype(BF16)

    @pl.when(jnp.logical_not(is_gate) & has_mm & is_f32_blk)
    def _():
        slot = jnp.where(j == N_GATE_BLK + 1, SLOT_F, SLOT_VB)
        pf_ref[slot, POOL_HALO:POOL_HALO + TM, :] = _dot(h_ref[...], w_ref[...])

    @pl.when(has_mix & (j == N_GATE_BLK - 1))
    def _():
        pf_ref[SLOT_VB, 0:POOL_HALO, :] = pf_ref[SLOT_VB, TM:TM + POOL_HALO, :]

        @pl.when(jnp.min(minb_ref[...]) < -SAFE_CHUNK_LOG_DECAY)
        def _():
            st_ref[...] = snap_ref[...]
            lb = lb_ref[...]
            tri, causal, col = masks()
            hn = hn_ref[...]

            def redo(ci, carry):
                for _ in _hgrn_chunk_stages(
                        pl.multiple_of(ci * CHUNK, CHUNK), lb, 1.0 - lb, hn, tri, causal, col,
                        pb_ref, pf_ref, ya_ref, st_ref, [],
                        direct_refs=(a_ref, bs_ref, qs_ref, ks_ref)):
                    pass
                return carry

            lax.fori_loop(0, TM // CHUNK, redo, 0)


def _proj_mix_call(layer, h, coef, lower_bounds, hgrn_norm, pool_scale, pool_w_b, w_b,
                   w_in_f32, seq):
    t = h.shape[0]
    nt = t // TM
    cast_next = layer + 1 < DEPTH
    n_cast_blk = D_MODEL // CAST_ROWS
    assert n_cast_blk <= nt * N_COL_BLK

    def h_map(i, j):
        return (jnp.minimum(i, nt - 1), 0)

    def w_map(i, j):
        cb = (j + FIRST_GATE_BLK) % N_COL_BLK
        return (0, jnp.where(i == nt, FIRST_GATE_BLK - 1, cb))

    def cast_row(i, j):
        return jnp.minimum(i * N_COL_BLK + j, n_cast_blk - 1)

    def g_map(i, j):
        gb = jnp.minimum(j, N_GATE_BLK - 1)
        return (jnp.minimum(i, nt - 1), jnp.where(i == nt, N_GATE_BLK - 1, gb))

    def y_map(i, j):
        return (jnp.maximum(i - 1, 0), 0)

    const3 = lambda i, j: (layer, 0, 0)
    in_specs = [
        pl.BlockSpec((None, DEPTH, A_WIDTH), const3),
        pl.BlockSpec((DEPTH, A_WIDTH), lambda i, j: (0, 0)),
        pl.BlockSpec((None, 1, A_WIDTH), const3),
        pl.BlockSpec((None, 1, B_WIDTH), const3),
        pl.BlockSpec((None, len(POOL_WINDOWS), POOL_GW, POOL_GW),
                     lambda i, j: (layer, 0, 0, 0)),
        pl.BlockSpec((TM, D_MODEL), h_map),
        pl.BlockSpec((D_MODEL, COL_BLK), w_map),
    ]
    out_shape = [jax.ShapeDtypeStruct((t, 2 * D_MODEL), BF16),
                 jax.ShapeDtypeStruct((t, A_WIDTH), BF16),
                 jax.ShapeDtypeStruct((t, B_WIDTH), BF16)]
    out_specs = [pl.BlockSpec((TM, COL_BLK), g_map),
                 pl.BlockSpec((TM, A_WIDTH), y_map),
                 pl.BlockSpec((TM, B_WIDTH), y_map)]
    operands = [coef, lower_bounds, hgrn_norm.reshape(DEPTH, 1, A_WIDTH),
                pool_scale.reshape(DEPTH, 1, B_WIDTH), pool_w_b, h, w_b]
    if cast_next:
        in_specs.append(pl.BlockSpec((None, CAST_ROWS, IN_COLS),
                                     lambda i, j: (layer + 1, cast_row(i, j), 0)))
        out_shape.append(jax.ShapeDtypeStruct((D_MODEL, IN_COLS), BF16))
        out_specs.append(pl.BlockSpec((CAST_ROWS, IN_COLS), lambda i, j: (cast_row(i, j), 0)))
        operands.append(w_in_f32)
    return pl.pallas_call(
        partial(_proj_mix_kernel, seq // TM, cast_next),
        out_shape=tuple(out_shape),
        grid=(nt + 1, N_COL_BLK),
        in_specs=in_specs,
        out_specs=tuple(out_specs),
        scratch_shapes=[
            pltpu.VMEM((4, TM, COL_BLK), BF16),
            pltpu.VMEM((2, TM + POOL_HALO, COL_BLK), F32),
            pltpu.VMEM((N_HEADS, HEAD_DIM, HEAD_DIM), F32),
            pltpu.VMEM((N_HEADS, HEAD_DIM, HEAD_DIM), F32),
            pltpu.VMEM((1, A_WIDTH), F32),
            pltpu.VMEM((1, A_WIDTH), F32),
            pltpu.VMEM((N_HEADS, CHUNK, CHUNK), F32),
            pltpu.VMEM((CHUNK, A_WIDTH), F32),
            pltpu.VMEM((CHUNK, A_WIDTH), F32),
            pltpu.VMEM((CHUNK, A_WIDTH), F32),
        ],
        compiler_params=pltpu.CompilerParams(
            dimension_semantics=("arbitrary", "arbitrary"),
            vmem_limit_bytes=VMEM_LIMIT),
        name="proj_mix",
    )(*operands)


def _out_kernel(emit_h, ya_ref, yb_ref, ga_ref, gb_ref, x_ref, epi_ref,
                wpa_ref, wpb_ref, wo_ref, o_ref, *rest):
    h_ref = rest[0] if emit_h else None
    out_ref = rest[-1]
    i = pl.program_id(0)
    n = pl.num_programs(0) - 1
    n_slices = D_MODEL // OUT_COL_SLICE
    piece = OUT_TM // (2 * n_slices)

    def epilogue_rows(r, gain1, gain2):
        rs = slice(r * piece, (r + 1) * piece)
        out = out_ref[rs, :]
        inv = lax.rsqrt(jnp.mean(out * out, axis=-1, keepdims=True) + EPS)
        xn = x_ref[rs, :] + (out * inv) * gain1
        o_ref[rs, :] = xn
        last = xn
        if emit_h:
            inv2 = lax.rsqrt(jnp.mean(xn * xn, axis=-1, keepdims=True) + EPS)
            last = (xn * inv2) * gain2 + epi_ref[EPI_NEXT_SHIFT]
            h_ref[rs, :] = last.astype(BF16)
        return _zero_row(OUT_COL_SLICE, [last])

    def step(with_matmuls, with_epilogue):
        if with_epilogue:
            gain1 = epi_ref[EPI_GATE] * epi_ref[EPI_POST_GAIN]
            gain2 = epi_ref[EPI_NEXT_GAIN] * (1.0 + epi_ref[EPI_NEXT_SCALE])
        products = []
        for which, (y_ref, w_ref) in enumerate(((ya_ref, wpa_ref), (yb_ref, wpb_ref))):
            parts = []
            for s in range(n_slices):
                anchor = None
                if with_epilogue:
                    anchor = epilogue_rows(which * n_slices + s, gain1, gain2)
                if with_matmuls:
                    cs = slice(s * OUT_COL_SLICE, (s + 1) * OUT_COL_SLICE)
                    part = _dot(y_ref[...], w_ref[:, cs])
                    parts.append(part if anchor is None else part + anchor)
            products.append(parts)
        if with_matmuls:
            merged = []
            for s in range(n_slices):
                cs = slice(s * OUT_COL_SLICE, (s + 1) * OUT_COL_SLICE)
                merged.append(_sigmoid(ga_ref[:, cs]) * products[0][s].astype(BF16)
                              + _sigmoid(gb_ref[:, cs]) * products[1][s].astype(BF16))
            out_ref[...] = _dot(jnp.concatenate(merged, axis=1), wo_ref[...])

    @pl.when(i == 0)
    def _():
        step(with_matmuls=True, with_epilogue=False)

    @pl.when((i > 0) & (i < n))
    def _():
        step(with_matmuls=True, with_epilogue=True)

    @pl.when(i == n)
    def _():
        step(with_matmuls=False, with_epilogue=True)


def _out_call(layer, ya, yb, g, x2, epi, wpa_b, wpb_b, wo_b, seq):
    t = x2.shape[0]
    n = t // OUT_TM
    tpb = seq // OUT_TM
    emit_h = layer + 1 < DEPTH
    resident = dict(pipeline_mode=pl.Buffered(1))

    def cur_spec(width, cb=0):
        return pl.BlockSpec((OUT_TM, width), lambda i: (jnp.minimum(i, n - 1), cb))

    def prev_spec(width):
        return pl.BlockSpec((OUT_TM, width), lambda i: (jnp.maximum(i - 1, 0), 0))

    epi_spec = pl.BlockSpec((None, None, N_EPI_ROWS, 1, D_MODEL),
                            lambda i: (layer, jnp.maximum(i - 1, 0) // tpb, 0, 0, 0))

    out_shape = [jax.ShapeDtypeStruct((t, D_MODEL), F32)]
    out_specs = [prev_spec(D_MODEL)]
    if emit_h:
        out_shape.append(jax.ShapeDtypeStruct((t, D_MODEL), BF16))
        out_specs.append(prev_spec(D_MODEL))
    res = pl.pallas_call(
        partial(_out_kernel, emit_h),
        out_shape=tuple(out_shape),
        grid=(n + 1,),
        in_specs=[
            cur_spec(A_WIDTH), cur_spec(B_WIDTH),
            cur_spec(D_MODEL, 0), cur_spec(D_MODEL, 1),
            prev_spec(D_MODEL),
            epi_spec,
            pl.BlockSpec((None, A_WIDTH, D_MODEL), lambda i: (layer, 0, 0), **resident),
            pl.BlockSpec((None, B_WIDTH, D_MODEL), lambda i: (layer, 0, 0), **resident),
            pl.BlockSpec((None, D_MODEL, D_MODEL), lambda i: (layer, 0, 0), **resident),
        ],
        out_specs=tuple(out_specs),
        scratch_shapes=[pltpu.VMEM((OUT_TM, D_MODEL), F32)],
        compiler_params=pltpu.CompilerParams(
            dimension_semantics=("arbitrary",),
            vmem_limit_bytes=VMEM_LIMIT,
            allow_input_fusion=[False] * 6 + [True] * 3),
        name="out_stage",
    )(ya, yb, g, g, x2, epi, wpa_b, wpb_b, wo_b)
    return res if emit_h else (res[0], None)


def kernel(x, c, w_ada, b_ada, norm_pre, norm_post, w_in, lower_bounds, hgrn_norm,
           pool_w, pool_scale, w_proj_a, w_proj_b, w_out):
    bsz, seq, d = x.shape
    assert d == D_MODEL and seq % TM == 0 and seq % OUT_TM == 0 and seq % NORM_TM == 0
    wpa_b = w_proj_a.astype(BF16)
    wpb_b = w_proj_b.astype(BF16)
    wo_b = w_out.astype(BF16)
    pool_w_b = pool_w.astype(BF16)

    mod = _ada_call(c, w_ada, b_ada)
    mod5 = mod.reshape(DEPTH, bsz, 3, 1, D_MODEL)
    jj = jnp.arange(DEPTH)
    coef = ((jj[None, :] >= 1) & (jj[None, :] <= jj[:, None])).astype(F32)
    coef = jnp.broadcast_to(coef[:, :, None], (DEPTH, DEPTH, A_WIDTH))

    nxt = jnp.minimum(jnp.arange(DEPTH) + 1, DEPTH - 1)
    per_batch = lambda v: jnp.broadcast_to(v[:, None, None, :], (DEPTH, bsz, 1, D_MODEL))
    epi = jnp.stack([mod5[:, :, 2], per_batch(norm_post), per_batch(norm_pre[nxt]),
                     mod5[nxt, :, 0], mod5[nxt, :, 1]], axis=2)

    x2 = x.reshape(bsz * seq, D_MODEL)
    h = _norm_call(x2, mod5, norm_pre, seq)
    w_b = w_in[0].astype(BF16)
    for layer in range(DEPTH):
        g, ya, yb, *w_next = _proj_mix_call(layer, h, coef, lower_bounds, hgrn_norm, pool_scale,
                                            pool_w_b, w_b, w_in, seq)
        if w_next:
            w_b = w_next[0]
        x2, h = _out_call(layer, ya, yb, g, x2, epi, wpa_b, wpb_b, wo_b, seq)
    return x2.reshape(bsz, seq, D_MODEL)
```
